```python
import math
import jax, jax.numpy as jnp
from jax import lax
import numpy as np

D_MODEL = 1024
BATCH = 2
SEQ = 8192
DEPTH = 2

HEAD_DIM = 64
N_HEADS = D_MODEL // HEAD_DIM
SB_HEADS = N_HEADS // 4
SWA_HEADS = N_HEADS // 2
SWA_KV_HEADS = SWA_HEADS // 4
MOBA_HEADS = N_HEADS - SB_HEADS - SWA_HEADS
SB_W = SB_HEADS * HEAD_DIM
SWA_W = SWA_HEADS * HEAD_DIM
SWA_KV_W = SWA_KV_HEADS * HEAD_DIM
MOBA_W = MOBA_HEADS * HEAD_DIM
MIX_W = SB_W + SWA_W + MOBA_W
IN_W = 3 * SB_W + SWA_W + 2 * SWA_KV_W + 3 * MOBA_W
Q_BLOCK = 128
WINDOW = 128
MOBA_BLOCK = 256
MOBA_TOPK = 3
NUM_BUCKETS = 32
MAX_DISTANCE = 128
BIAS_HEADS = SWA_HEADS + MOBA_HEADS
N_EXPERTS = 32
TOP_K = 4
D_FF = D_MODEL
SWIGLU_LIMIT = 7.0
SWIGLU_ALPHA = 1.702
MOE_BLOCK = 256
EPS = 1e-6
NEG = -1e30

kernel_name = "hymba_style_sb_swa_moba_moe_trunk"


def rms_norm(x, g):
    xf = x.astype(jnp.float32)
    y = xf * lax.rsqrt(jnp.mean(xf * xf, axis=-1, keepdims=True) + EPS)
    return (y * g.astype(jnp.float32)).astype(x.dtype)


def t5_bucket(dist):
    n = jnp.maximum(dist, 0)
    max_exact = NUM_BUCKETS // 2
    nf = jnp.maximum(n, 1).astype(jnp.float32)
    large = max_exact + (jnp.log(nf / max_exact) / math.log(MAX_DISTANCE / max_exact)
                         * (NUM_BUCKETS - max_exact)).astype(jnp.int32)
    large = jnp.minimum(large, NUM_BUCKETS - 1)
    return jnp.where(n < max_exact, n, large)


def stick_breaking_attention(q, k, v):
    B, H, S, d = q.shape
    nq = S // Q_BLOCK
    scale = d ** -0.5
    qb = q.reshape(B, H, nq, Q_BLOCK, d).transpose(2, 0, 1, 3, 4)
    key_pos = jnp.arange(S)

    def block(args):
        i, qi = args
        t = i * Q_BLOCK + jnp.arange(Q_BLOCK)
        z = jnp.einsum('bhqd,bhkd->bhqk', qi, k, preferred_element_type=jnp.float32) * scale
        strict = key_pos[None, :] < t[:, None]
        log_1m = jnp.where(strict, jax.nn.log_sigmoid(-z), 0.0)
        later = lax.cumsum(log_1m, axis=3, reverse=True) - log_1m
        a = jnp.where(strict, jnp.exp(jax.nn.log_sigmoid(z) + later), 0.0)
        return jnp.einsum('bhqk,bhkd->bhqd', a.astype(v.dtype), v)

    out = lax.map(block, (jnp.arange(nq), qb))
    return out.transpose(1, 0, 3, 2, 4).reshape(B, S, H * d)


def sliding_window_attention(q, k, v, sinks, table):
    B, S, Hq, d = q.shape
    Hkv = k.shape[2]
    G = Hq // Hkv
    nq = S // Q_BLOCK
    scale = d ** -0.5
    qb = q.reshape(B, nq, Q_BLOCK, Hkv, G, d)

    def banded(a):
        ap = jnp.pad(a, ((0, 0), (Q_BLOCK, 0), (0, 0), (0, 0))).reshape(B, nq + 1, Q_BLOCK, Hkv, d)
        return jnp.concatenate([ap[:, :-1], ap[:, 1:]], axis=2)

    kw, vw = banded(k), banded(v)
    i = jnp.arange(Q_BLOCK)[:, None]
    j = jnp.arange(2 * Q_BLOCK)[None, :]
    dist = Q_BLOCK + i - j
    key_pos = (jnp.arange(nq)[:, None, None] - 1) * Q_BLOCK + j
    valid = (dist >= 0) & (dist < WINDOW) & (key_pos >= 0)
    bias = table[t5_bucket(dist)].transpose(2, 0, 1).reshape(Hkv, G, Q_BLOCK, 2 * Q_BLOCK)
    s = jnp.einsum('bnqhgd,bnkhd->bnhgqk', qb, kw, preferred_element_type=jnp.float32) * scale
    s = jnp.where(valid[None, :, None, None], s + bias.astype(jnp.float32), NEG)
    sink = jnp.broadcast_to(sinks.astype(jnp.float32).reshape(1, 1, Hkv, G, 1, 1), s.shape[:-1] + (1,))
    p = jax.nn.softmax(jnp.concatenate([s, sink], axis=-1), axis=-1)[..., :-1]
    out = jnp.einsum('bnhgqk,bnkhd->bnqhgd', p.astype(v.dtype), vw)
    return out.reshape(B, S, Hq * d)


def moba_attention(q, k, v, table):
    B, H, S, d = q.shape
    nq = S // Q_BLOCK
    nb = -(-S // MOBA_BLOCK)
    n_sel = min(MOBA_TOPK, nb)
    scale = d ** -0.5
    padlen = nb * MOBA_BLOCK - S
    kb = jnp.pad(k, ((0, 0), (0, 0), (0, padlen), (0, 0))).reshape(B, H, nb, MOBA_BLOCK, d)
    vb = jnp.pad(v, ((0, 0), (0, 0), (0, padlen), (0, 0))).reshape(B, H, nb, MOBA_BLOCK, d)
    k_mean = jnp.mean(kb.astype(jnp.float32), axis=3).astype(k.dtype)
    qb = q.reshape(B, H, nq, Q_BLOCK, d).transpose(2, 0, 1, 3, 4)
    bi = jnp.arange(B)[:, None, None, None]
    hi = jnp.arange(H)[None, :, None, None]
    table_t = table.T
    j = jnp.arange(MOBA_BLOCK)

    def block(args):
        i, qi = args
        t = i * Q_BLOCK + jnp.arange(Q_BLOCK)
        cur = (i * Q_BLOCK) // MOBA_BLOCK
        gate = jnp.einsum('bhqd,bhnd->bhqn', qi, k_mean, preferred_element_type=jnp.float32)
        gate = jnp.where(jnp.arange(nb) < cur, gate, NEG)
        _, sel = lax.top_k(gate, n_sel)
        sel_ok = sel < cur
        kg = kb[bi, hi, sel]
        vg = vb[bi, hi, sel]
        sg = jnp.einsum('bhqd,bhqnkd->bhqnk', qi, kg, preferred_element_type=jnp.float32) * scale
        dist_g = t[None, None, :, None, None] - (sel[..., None] * MOBA_BLOCK + j)
        sg = sg + table_t[hi[..., None], t5_bucket(dist_g)].astype(jnp.float32)
        sg = jnp.where(sel_ok[..., None], sg, NEG).reshape(B, H, Q_BLOCK, n_sel * MOBA_BLOCK)
        ko = lax.dynamic_index_in_dim(kb, cur, axis=2, keepdims=False)
        vo = lax.dynamic_index_in_dim(vb, cur, axis=2, keepdims=False)
        so = jnp.einsum('bhqd,bhkd->bhqk', qi, ko, preferred_element_type=jnp.float32) * scale
        dist_o = t[:, None] - (cur * MOBA_BLOCK + j)[None, :]
        so = so + table_t[:, t5_bucket(dist_o)].astype(jnp.float32)
        so = jnp.where(dist_o >= 0, so, NEG)
        p = jax.nn.softmax(jnp.concatenate([sg, so], axis=-1), axis=-1)
        pg = p[..., :n_sel * MOBA_BLOCK].reshape(B, H, Q_BLOCK, n_sel, MOBA_BLOCK)
        po = p[..., n_sel * MOBA_BLOCK:]
        return (jnp.einsum('bhqnk,bhqnkd->bhqd', pg.astype(v.dtype), vg)
                + jnp.einsum('bhqk,bhkd->bhqd', po.astype(v.dtype), vo))

    out = lax.map(block, (jnp.arange(nq), qb))
    return out.transpose(1, 0, 3, 2, 4).reshape(B, S, H * d)


def token_mixer(h, w_in, w_out, g_group, sinks, rel_bias):
    B, S, _ = h.shape
    proj = h @ w_in
    widths = [SB_W] * 3 + [SWA_W, SWA_KV_W, SWA_KV_W] + [MOBA_W] * 3
    cuts = [int(c) for c in np.cumsum(widths)[:-1]]
    sb_q, sb_k, sb_v, sw_q, sw_k, sw_v, mb_q, mb_k, mb_v = jnp.split(proj, cuts, axis=-1)

    def heads(a, n):
        return a.reshape(B, S, n, HEAD_DIM).transpose(0, 2, 1, 3)

    o_sb = stick_breaking_attention(heads(sb_q, SB_HEADS), heads(sb_k, SB_HEADS), heads(sb_v, SB_HEADS))
    o_sw = sliding_window_attention(sw_q.reshape(B, S, SWA_HEADS, HEAD_DIM),
                                    sw_k.reshape(B, S, SWA_KV_HEADS, HEAD_DIM),
                                    sw_v.reshape(B, S, SWA_KV_HEADS, HEAD_DIM),
                                    sinks, rel_bias[:, :SWA_HEADS])
    o_mb = moba_attention(heads(mb_q, MOBA_HEADS), heads(mb_k, MOBA_HEADS), heads(mb_v, MOBA_HEADS),
                          rel_bias[:, SWA_HEADS:])
    merged = jnp.concatenate([
        rms_norm(o_sb, g_group[:SB_W]),
        rms_norm(o_sw, g_group[SB_W:SB_W + SWA_W]),
        rms_norm(o_mb, g_group[SB_W + SWA_W:]),
    ], axis=-1)
    return merged @ w_out


def moe_ffn(h, w_router, b_router, w_gate_up, b_gate_up, w_down, b_down):
    B, S, D = h.shape
    n_tok = B * S
    hf = h.reshape(n_tok, D)
    logits = jnp.dot(hf, w_router, preferred_element_type=jnp.float32) + b_router.astype(jnp.float32)
    top_logit, top_e = lax.top_k(logits, TOP_K)
    weights = jax.nn.softmax(top_logit, axis=-1).astype(h.dtype)
    n_assign = n_tok * TOP_K
    flat_e = top_e.reshape(n_assign)
    flat_tok = jnp.arange(n_assign, dtype=jnp.int32) // TOP_K
    order = jnp.argsort(flat_e)
    e_sorted = flat_e[order]
    counts = jnp.bincount(flat_e, length=N_EXPERTS)
    padded = (counts + MOE_BLOCK - 1) // MOE_BLOCK * MOE_BLOCK
    start = jnp.cumsum(counts) - counts
    pad_end = jnp.cumsum(padded)
    pad_start = pad_end - padded
    dest = pad_start[e_sorted] + jnp.arange(n_assign) - start[e_sorted]
    n_rows = (-(-n_assign // MOE_BLOCK) + N_EXPERTS) * MOE_BLOCK
    n_blocks = n_rows // MOE_BLOCK
    row_tok = jnp.zeros((n_rows,), jnp.int32).at[dest].set(flat_tok[order])
    row_w = jnp.zeros((n_rows,), h.dtype).at[dest].set(weights.reshape(n_assign)[order])
    block_e = jnp.minimum(jnp.searchsorted(pad_end, jnp.arange(n_blocks) * MOE_BLOCK, side='right'),
                          N_EXPERTS - 1)
    xs = hf[row_tok].reshape(n_blocks, MOE_BLOCK, D)

    def expert_block(args):
        xb, e = args
        gu = xb @ w_gate_up[e] + b_gate_up[e]
        gate = jnp.minimum(gu[:, :D_FF], SWIGLU_LIMIT)
        up = jnp.clip(gu[:, D_FF:], -SWIGLU_LIMIT, SWIGLU_LIMIT)
        act = (up + 1.0) * gate * jax.nn.sigmoid(SWIGLU_ALPHA * gate)
        return act @ w_down[e] + b_down[e]

    ys = lax.map(expert_block, (xs, block_e)).reshape(n_rows, D)
    out = jnp.zeros((n_tok, D), h.dtype).at[row_tok].add(ys * row_w[:, None])
    return out.reshape(B, S, D)


def setup_inputs(seed: int = 0) -> dict:
    key = jax.random.key(seed)
    ks = jax.random.split(key, 20)
    f32 = jnp.float32
    nrm = lambda k, shape, s: jax.random.normal(k, shape, f32) * s
    return {
        "x": nrm(ks[0], (BATCH, SEQ, D_MODEL), 1.0),
        "c": nrm(ks[1], (BATCH, D_MODEL), 1.0),
        "w_in": nrm(ks[2], (DEPTH, D_MODEL, IN_W), D_MODEL ** -0.5),
        "w_out": nrm(ks[3], (DEPTH, MIX_W, D_MODEL), MIX_W ** -0.5),
        "g_norm_mix": 1.0 + nrm(ks[4], (DEPTH, D_MODEL), 0.1),
        "g_norm_ffn": 1.0 + nrm(ks[5], (DEPTH, D_MODEL), 0.1),
        "g_group": 1.0 + nrm(ks[6], (DEPTH, MIX_W), 0.1),
        "w_mod": nrm(ks[7], (DEPTH, D_MODEL, 6 * D_MODEL), 0.5 * D_MODEL ** -0.5),
        "b_mod": nrm(ks[8], (DEPTH, 6 * D_MODEL), 0.02),
        "swa_sinks": nrm(ks[9], (DEPTH, SWA_HEADS), 1.0),
        "rel_bias": nrm(ks[10], (NUM_BUCKETS, BIAS_HEADS), 0.5),
        "w_router": nrm(ks[11], (DEPTH, D_MODEL, N_EXPERTS), D_MODEL ** -0.5),
        "b_router": nrm(ks[12], (DEPTH, N_EXPERTS), 0.01),
        "w_gate_up": nrm(ks[13], (DEPTH, N_EXPERTS, D_MODEL, 2 * D_FF), D_MODEL ** -0.5),
        "b_gate_up": nrm(ks[14], (DEPTH, N_EXPERTS, 2 * D_FF), 0.02),
        "w_down": nrm(ks[15], (DEPTH, N_EXPERTS, D_FF, D_MODEL), D_FF ** -0.5),
        "b_down": nrm(ks[16], (DEPTH, N_EXPERTS, D_MODEL), 0.02),
        "g_final": 1.0 + nrm(ks[17], (D_MODEL,), 0.1),
    }


def reference(x, c, w_in, w_out, g_norm_mix, g_norm_ffn, g_group, w_mod, b_mod, swa_sinks,
              rel_bias, w_router, b_router, w_gate_up, b_gate_up, w_down, b_down, g_final):
    c_act = jax.nn.silu(c)
    for l in range(DEPTH):
        mod = (c_act @ w_mod[l] + b_mod[l])[:, None, :]
        sh1, sc1, gt1, sh2, sc2, gt2 = jnp.split(mod, 6, axis=-1)
        h = rms_norm(x, g_norm_mix[l]) * (1.0 + sc1) + sh1
        x = x + gt1 * token_mixer(h, w_in[l], w_out[l], g_group[l], swa_sinks[l], rel_bias)
        h = rms_norm(x, g_norm_ffn[l]) * (1.0 + sc2) + sh2
        x = x + gt2 * moe_ffn(h, w_router[l], b_router[l], w_gate_up[l], b_gate_up[l],
                              w_down[l], b_down[l])
    return rms_norm(x, g_final)
```

```python
import functools
import math

import jax
import jax.numpy as jnp
from jax import lax
from jax.experimental import pallas as pl
from jax.experimental.pallas import tpu as pltpu

F32 = jnp.float32
BF16 = jnp.bfloat16

HEAD_DIM = 64
SB_HEADS = 4
SWA_HEADS = 8
SWA_KV_HEADS = 2
MOBA_HEADS = 4
SB_W = SB_HEADS * HEAD_DIM
SWA_W = SWA_HEADS * HEAD_DIM
SWA_KV_W = SWA_KV_HEADS * HEAD_DIM
MOBA_W = MOBA_HEADS * HEAD_DIM
Q_BLOCK = 128
WINDOW = 128
MOBA_BLOCK = 256
MOBA_TOPK = 3
NUM_BUCKETS = 32
MAX_DISTANCE = 128
N_EXPERTS = 32
TOP_K = 4
SWIGLU_LIMIT = 7.0
SWIGLU_ALPHA = 1.702
MOE_BLOCK = 256
EPS = 1e-6
NEG = -1e30
ATTN_SCALE = HEAD_DIM ** -0.5

LANES = 128
COL_SB_Q, COL_SB_K, COL_SB_V = 0, 2, 4
COL_SW_Q, COL_SW_K, COL_SW_V = 6, 10, 11
COL_MB_Q, COL_MB_K, COL_MB_V = 12, 14, 16
IN_W = 18 * LANES

TOKEN_TILE = 512
COMBINE_TILE = 128
VMEM_LIMIT = 56 * 1024 * 1024


def _rms(x, g):
    return x * lax.rsqrt(jnp.mean(x * x, axis=-1, keepdims=True) + EPS) * g


def _dot_t(a, b):
    return lax.dot_general(a, b, (((1,), (1,)), ((), ())), preferred_element_type=F32)


def _dot(a, b):
    return jnp.dot(a, b, preferred_element_type=F32)


def _mod_kernel(c_ref, w_ref, b_ref, o_ref):
    c = c_ref[...]
    ca = c * (1.0 / (1.0 + jnp.exp(-c)))
    o_ref[0] = _dot(ca, w_ref[0]) + b_ref[0]


def _modulation(c, w_mod, b_mod):
    depth, d, six_d = w_mod.shape
    b = c.shape[0]
    rows = 8
    c8 = jnp.zeros((rows, d), F32).at[:b].set(c)
    tn = six_d // 6
    out = pl.pallas_call(
        _mod_kernel,
        grid=(depth, six_d // tn),
        in_specs=[
            pl.BlockSpec((rows, d), lambda l, j: (0, 0)),
            pl.BlockSpec((1, d, tn), lambda l, j: (l, 0, j)),
            pl.BlockSpec((1, 1, tn), lambda l, j: (l, 0, j)),
        ],
        out_specs=pl.BlockSpec((1, rows, tn), lambda l, j: (l, 0, j)),
        out_shape=jax.ShapeDtypeStruct((depth, rows, six_d), F32),
        name="adaln_mod",
    )(c8, w_mod, b_mod.reshape(depth, 1, six_d))
    return out[:, :b].reshape(depth, b, 6, d)


def _inproj_kernel(x_ref, g_ref, m_ref, w_ref, o_ref):
    m = m_ref[0]
    h = _rms(x_ref[...], g_ref[...]) * (1.0 + m[1:2]) + m[0:1]
    hb = h.astype(BF16)
    step = 2 * LANES
    for j in range(IN_W // step):
        o_ref[:, j * step:(j + 1) * step] = _dot(hb, w_ref[:, j * step:(j + 1) * step]).astype(BF16)


def _inproj(x, g, mods, w_in_b, seq):
    n, d = x.shape
    tm = min(TOKEN_TILE, seq)
    per_batch = seq // tm
    return pl.pallas_call(
        _inproj_kernel,
        grid=(n // tm,),
        in_specs=[
            pl.BlockSpec((tm, d), lambda i: (i, 0)),
            pl.BlockSpec((1, d), lambda i: (0, 0)),
            pl.BlockSpec((1, 6, d), lambda i: (i // per_batch, 0, 0)),
            pl.BlockSpec((d, IN_W), lambda i: (0, 0)),
        ],
        out_specs=pl.BlockSpec((tm, IN_W), lambda i: (i, 0)),
        out_shape=jax.ShapeDtypeStruct((n, IN_W), BF16),
        compiler_params=pltpu.CompilerParams(dimension_semantics=("arbitrary",), vmem_limit_bytes=VMEM_LIMIT),
        name="norm_inproj",
    )(x, g.reshape(1, d), mods, w_in_b)


def _sb_kernel(q_ref, k_ref, v_ref, o_ref):
    i = pl.program_id(2)
    tq = q_ref.shape[0]
    lane = lax.broadcasted_iota(jnp.int32, (tq, LANES), 1)
    row = lax.broadcasted_iota(jnp.int32, (tq, LANES), 0)
    strict = lane < row
    tri = jnp.where(row > lane, 1.0, 0.0).astype(BF16)
    half = jnp.concatenate([tri, jnp.ones((LANES, LANES), BF16)], axis=1)
    scan_w = jnp.concatenate([half, half], axis=0)
    q = q_ref[...] * ATTN_SCALE

    def chunk(c, qm, carry, acc, diagonal):
        start = pl.multiple_of(c * LANES, LANES)
        kc = k_ref[pl.ds(start, LANES), :]
        vc = v_ref[pl.ds(start, LANES), :]
        z = _dot_t(qm, kc)
        sp = jnp.maximum(z, 0.0) + jnp.log(1.0 + jnp.exp(-jnp.abs(z)))
        log_1m = -sp
        if diagonal:
            log_1m = jnp.where(strict, log_1m, 0.0)
        hi = log_1m.astype(BF16)
        lo = (log_1m - hi.astype(F32)).astype(BF16)
        cs = _dot(jnp.concatenate([hi, lo], axis=1), scan_w)
        a = jnp.exp((z - sp) + (cs[:, :LANES] + carry))
        if diagonal:
            a = jnp.where(strict, a, 0.0)
        acc = acc + _dot(a.astype(BF16), vc)
        return carry + cs[:, LANES:], acc

    outs = []
    for hh in range(2):
        qm = jnp.where((lane >= HEAD_DIM * hh) & (lane < HEAD_DIM * (hh + 1)), q, jnp.zeros_like(q))
        zero = jnp.zeros((tq, LANES), F32)
        state = chunk(i, qm, zero, zero, True)
        state = lax.fori_loop(0, i, lambda jj, st, qm=qm: chunk(i - 1 - jj, qm, st[0], st[1], False), state)
        outs.append(state[1])
    o_ref[...] = jnp.where(lane < HEAD_DIM, outs[0], outs[1])


def _sb_attention(proj, batch, seq):
    n = proj.shape[0]
    nq = seq // Q_BLOCK
    pairs = SB_HEADS // 2
    return pl.pallas_call(
        _sb_kernel,
        grid=(batch, pairs, nq),
        in_specs=[
            pl.BlockSpec((Q_BLOCK, LANES), lambda b, p, i: (b * nq + i, COL_SB_Q + p)),
            pl.BlockSpec((seq, LANES), lambda b, p, i: (b, COL_SB_K + p)),
            pl.BlockSpec((seq, LANES), lambda b, p, i: (b, COL_SB_V + p)),
        ],
        out_specs=pl.BlockSpec((Q_BLOCK, LANES), lambda b, p, i: (b * nq + i, p)),
        out_shape=jax.ShapeDtypeStruct((n, SB_W), F32),
        compiler_params=pltpu.CompilerParams(
            dimension_semantics=("arbitrary", "arbitrary", "arbitrary"), vmem_limit_bytes=VMEM_LIMIT),
        name="sb_attention",
    )(proj, proj, proj)


def _swa_kernel(sink_ref, qa_ref, qb_ref, kp_ref, kc_ref, vp_ref, vc_ref, bias_ref, o_ref):
    i = pl.program_id(1)
    kk = jnp.concatenate([kp_ref[...], kc_ref[...]], axis=0)
    vv = jnp.concatenate([vp_ref[...], vc_ref[...]], axis=0)
    r = lax.broadcasted_iota(jnp.int32, (Q_BLOCK, 2 * Q_BLOCK), 0)
    j = lax.broadcasted_iota(jnp.int32, (Q_BLOCK, 2 * Q_BLOCK), 1)
    dist = Q_BLOCK + r - j
    valid = (dist >= 0) & (dist < WINDOW) & ((j >= Q_BLOCK) | (i > 0))
    group = SWA_HEADS // SWA_KV_HEADS
    for h in range(SWA_HEADS):
        g = h // group
        q_ref = qa_ref if h < group else qb_ref
        c0 = HEAD_DIM * (h % group)
        qh = q_ref[:, c0:c0 + HEAD_DIM] * ATTN_SCALE
        s = _dot_t(qh, kk[:, HEAD_DIM * g:HEAD_DIM * (g + 1)]) + bias_ref[h]
        s = jnp.where(valid, s, NEG)
        sink = sink_ref[h]
        m = jnp.maximum(jnp.max(s, axis=-1, keepdims=True), sink)
        p = jnp.exp(s - m)
        l = jnp.sum(p, axis=-1, keepdims=True) + jnp.exp(sink - m)
        o = _dot(p.astype(BF16), vv[:, HEAD_DIM * g:HEAD_DIM * (g + 1)])
        o_ref[:, HEAD_DIM * h:HEAD_DIM * (h + 1)] = o / l


def _swa_attention(proj, sinks, bias, batch, seq):
    n = proj.shape[0]
    nq = seq // Q_BLOCK
    wide = 2 * LANES
    grid_spec = pltpu.PrefetchScalarGridSpec(
        num_scalar_prefetch=1,
        grid=(batch, nq),
        in_specs=[
            pl.BlockSpec((Q_BLOCK, wide), lambda b, i, s: (b * nq + i, COL_SW_Q // 2)),
            pl.BlockSpec((Q_BLOCK, wide), lambda b, i, s: (b * nq + i, COL_SW_Q // 2 + 1)),
            pl.BlockSpec((Q_BLOCK, LANES), lambda b, i, s: (b * nq + jnp.maximum(i - 1, 0), COL_SW_K)),
            pl.BlockSpec((Q_BLOCK, LANES), lambda b, i, s: (b * nq + i, COL_SW_K)),
            pl.BlockSpec((Q_BLOCK, LANES), lambda b, i, s: (b * nq + jnp.maximum(i - 1, 0), COL_SW_V)),
            pl.BlockSpec((Q_BLOCK, LANES), lambda b, i, s: (b * nq + i, COL_SW_V)),
            pl.BlockSpec((SWA_HEADS, Q_BLOCK, 2 * Q_BLOCK), lambda b, i, s: (0, 0, 0)),
        ],
        out_specs=pl.BlockSpec((Q_BLOCK, SWA_W), lambda b, i, s: (b * nq + i, 0)),
    )
    return pl.pallas_call(
        _swa_kernel,
        grid_spec=grid_spec,
        out_shape=jax.ShapeDtypeStruct((n, SWA_W), F32),
        compiler_params=pltpu.CompilerParams(dimension_semantics=("arbitrary", "arbitrary")),
        name="swa_attention",
    )(sinks, proj, proj, proj, proj, proj, proj, bias)


def _moba_kernel(far_ref, q_ref, k_ref, v_ref, bown_ref, bprev_ref, o_ref, kmean_ref):
    p = pl.program_id(1)
    i = pl.program_id(2)
    seq = k_ref.shape[0]
    tq = q_ref.shape[0]
    nb = seq // MOBA_BLOCK
    lane = lax.broadcasted_iota(jnp.int32, (tq, LANES), 1)
    row = lax.broadcasted_iota(jnp.int32, (tq, LANES), 0)

    @pl.when(i == 0)
    def _():
        blk = lax.broadcasted_iota(jnp.int32, (LANES, seq), 0)
        pos = lax.broadcasted_iota(jnp.int32, (LANES, seq), 1)
        lo = blk * MOBA_BLOCK
        avg = jnp.where((pos >= lo) & (pos < lo + MOBA_BLOCK), 1.0 / MOBA_BLOCK, 0.0).astype(BF16)
        kmean_ref[...] = _dot(avg, k_ref[...]).astype(BF16)

    cur = (i * Q_BLOCK) // MOBA_BLOCK
    parity_row = (i * Q_BLOCK) % MOBA_BLOCK
    q = q_ref[...]
    kmean = kmean_ref[...]
    jw = lax.broadcasted_iota(jnp.int32, (tq, MOBA_BLOCK), 1)
    rw = lax.broadcasted_iota(jnp.int32, (tq, MOBA_BLOCK), 0)
    causal_own = (parity_row + rw) >= jw

    outs = []
    for hh in range(2):
        hmask = (lane >= HEAD_DIM * hh) & (lane < HEAD_DIM * (hh + 1))
        qg = jnp.where(hmask, q, jnp.zeros_like(q))
        qm = qg * ATTN_SCALE
        gate = jnp.where(lane < cur, _dot_t(qg, kmean), NEG)
        sel = jnp.zeros((tq, LANES), jnp.bool_)
        for _ in range(min(MOBA_TOPK, nb)):
            mx = jnp.max(gate, axis=-1, keepdims=True)
            first = jnp.min(jnp.where(gate == mx, lane, LANES), axis=-1, keepdims=True)
            hit = lane == first
            sel = sel | hit
            gate = jnp.where(hit, -jnp.inf, gate)
        sel_f = jnp.where(sel & (lane < cur), 1.0, 0.0).astype(BF16)

        def scores(n):
            start = pl.multiple_of(n * MOBA_BLOCK, MOBA_BLOCK)
            return _dot_t(qm, k_ref[pl.ds(start, MOBA_BLOCK), :]), v_ref[pl.ds(start, MOBA_BLOCK), :]

        def selected(n):
            pick = jnp.where(row == n, 1.0, 0.0).astype(BF16)
            rep = _dot(sel_f, pick) > 0.5
            return jnp.concatenate([rep, rep], axis=1)

        s, vb = scores(cur)
        s = jnp.where(causal_own, s + bown_ref[hh, 0], NEG)
        m = jnp.max(s, axis=-1, keepdims=True)
        pexp = jnp.exp(s - m)
        l = jnp.sum(pexp, axis=-1, keepdims=True)
        acc = _dot(pexp.astype(BF16), vb)

        def update(state, s, vb):
            m, l, acc = state
            m_new = jnp.maximum(m, jnp.max(s, axis=-1, keepdims=True))
            alpha = jnp.exp(m - m_new)
            pexp = jnp.exp(s - m_new)
            l = alpha * l + jnp.sum(pexp, axis=-1, keepdims=True)
            acc = alpha * acc + _dot(pexp.astype(BF16), vb)
            return m_new, l, acc

        def prev_block(state):
            s, vb = scores(cur - 1)
            s = jnp.where(selected(cur - 1), s + bprev_ref[hh, 0], NEG)
            return update(state, s, vb)

        state = lax.cond(cur >= 1, prev_block, lambda st: st, (m, l, acc))

        far_bias = far_ref[2 * p + hh]

        def far_block(n, state):
            s, vb = scores(n)
            s = jnp.where(selected(n), s + far_bias, NEG)
            return update(state, s, vb)

        m, l, acc = lax.fori_loop(0, jnp.maximum(cur - 1, 0), far_block, state)
        outs.append(acc / l)
    o_ref[...] = jnp.where(lane < HEAD_DIM, outs[0], outs[1])


def _moba_attention(proj, far_bias, bias_own, bias_prev, batch, seq):
    n = proj.shape[0]
    nq = seq // Q_BLOCK
    pairs = MOBA_HEADS // 2
    per_blk = MOBA_BLOCK // Q_BLOCK
    grid_spec = pltpu.PrefetchScalarGridSpec(
        num_scalar_prefetch=1,
        grid=(batch, pairs, nq),
        in_specs=[
            pl.BlockSpec((Q_BLOCK, LANES), lambda b, p, i, f: (b * nq + i, COL_MB_Q + p)),
            pl.BlockSpec((seq, LANES), lambda b, p, i, f: (b, COL_MB_K + p)),
            pl.BlockSpec((seq, LANES), lambda b, p, i, f: (b, COL_MB_V + p)),
            pl.BlockSpec((2, 1, Q_BLOCK, MOBA_BLOCK), lambda b, p, i, f: (p, i % per_blk, 0, 0)),
            pl.BlockSpec((2, 1, Q_BLOCK, MOBA_BLOCK), lambda b, p, i, f: (p, i % per_blk, 0, 0)),
        ],
        out_specs=pl.BlockSpec((Q_BLOCK, LANES), lambda b, p, i, f: (b * nq + i, p)),
        scratch_shapes=[pltpu.VMEM((LANES, LANES), BF16)],
    )
    return pl.pallas_call(
        _moba_kernel,
        grid_spec=grid_spec,
        out_shape=jax.ShapeDtypeStruct((n, MOBA_W), F32),
        compiler_params=pltpu.CompilerParams(
            dimension_semantics=("arbitrary", "arbitrary", "arbitrary"), vmem_limit_bytes=VMEM_LIMIT),
        name="moba_attention",
    )(far_bias, proj, proj, proj, bias_own, bias_prev)


def _outproj_kernel(osb_ref, osw_ref, omb_ref, x_ref, m_ref, gg_ref, wo_ref, gf_ref, wr_ref, br_ref,
                    xo_ref, h_ref, te_ref, tw_ref):
    m = m_ref[0]
    c1, c2 = SB_W, SB_W + SWA_W
    y = _dot(_rms(osb_ref[...], gg_ref[:, :c1]).astype(BF16), wo_ref[:c1, :])
    y = y + _dot(_rms(osw_ref[...], gg_ref[:, c1:c2]).astype(BF16), wo_ref[c1:c2, :])
    y = y + _dot(_rms(omb_ref[...], gg_ref[:, c2:]).astype(BF16), wo_ref[c2:, :])
    x = x_ref[...] + m[2:3] * y
    xo_ref[...] = x
    h = _rms(x, gf_ref[...]) * (1.0 + m[4:5]) + m[3:4]
    h_ref[...] = h
    logits = _dot(h.astype(BF16), wr_ref[...]) + br_ref[...]
    lane = lax.broadcasted_iota(jnp.int32, logits.shape, 1)
    ids = jnp.zeros(logits.shape, jnp.int32)
    wts = jnp.zeros(logits.shape, F32)
    top = None
    denom = None
    for r in range(TOP_K):
        mx = jnp.max(logits, axis=-1, keepdims=True)
        first = jnp.min(jnp.where(logits == mx, lane, LANES), axis=-1, keepdims=True)
        logits = jnp.where(lane == first, -jnp.inf, logits)
        if r == 0:
            top = mx
        e = jnp.exp(mx - top)
        denom = e if r == 0 else denom + e
        ids = jnp.where(lane == r, first, ids)
        wts = jnp.where(lane == r, e, wts)
    te_ref[...] = ids
    tw_ref[...] = wts / denom


def _outproj_router(o_sb, o_sw, o_mb, x, mods, g_group, w_out_b, g_ffn, w_router_b, b_router_p, seq):
    n, d = x.shape
    tm = min(TOKEN_TILE, seq)
    per_batch = seq // tm
    row = lambda i: (i, 0)
    const = lambda i: (0, 0)
    return pl.pallas_call(
        _outproj_kernel,
        grid=(n // tm,),
        in_specs=[
            pl.BlockSpec((tm, SB_W), row),
            pl.BlockSpec((tm, SWA_W), row),
            pl.BlockSpec((tm, MOBA_W), row),
            pl.BlockSpec((tm, d), row),
            pl.BlockSpec((1, 6, d), lambda i: (i // per_batch, 0, 0)),
            pl.BlockSpec((1, d), const),
            pl.BlockSpec((d, d), const),
            pl.BlockSpec((1, d), const),
            pl.BlockSpec((d, LANES), const),
            pl.BlockSpec((1, LANES), const),
        ],
        out_specs=[
            pl.BlockSpec((tm, d), row),
            pl.BlockSpec((tm, d), row),
            pl.BlockSpec((tm, LANES), row),
            pl.BlockSpec((tm, LANES), row),
        ],
        out_shape=[
            jax.ShapeDtypeStruct((n, d), F32),
            jax.ShapeDtypeStruct((n, d), F32),
            jax.ShapeDtypeStruct((n, LANES), jnp.int32),
            jax.ShapeDtypeStruct((n, LANES), F32),
        ],
        compiler_params=pltpu.CompilerParams(dimension_semantics=("arbitrary",), vmem_limit_bytes=VMEM_LIMIT),
        name="outproj_router",
    )(o_sb, o_sw, o_mb, x, mods, g_group.reshape(1, d), w_out_b, g_ffn.reshape(1, d), w_router_b, b_router_p)


def _moe_kernel(be_ref, nu_ref, tok_ref, h_hbm, wgu_ref, bgu_ref, wd_ref, bd_ref, o_ref,
                xbuf, wgu_b, wd_b, sem):
    i = pl.program_id(0)
    d_ff = wd_ref.shape[1]

    def row_copy(r):
        return pltpu.make_async_copy(h_hbm.at[pl.ds(tok_ref[0, 0, r], 1)], xbuf.at[pl.ds(r, 1)], sem)

    @pl.when(i < nu_ref[0])
    def _():
        def issue(r, carry):
            row_copy(r).start()
            return carry

        lax.fori_loop(0, MOE_BLOCK, issue, 0)

        changed = jnp.logical_or(i == 0, be_ref[i] != be_ref[jnp.maximum(i - 1, 0)])

        @pl.when(changed)
        def _():
            wgu_b[...] = wgu_ref[0].astype(BF16)
            wd_b[...] = wd_ref[0].astype(BF16)

        def wait(r, carry):
            row_copy(r).wait()
            return carry

        lax.fori_loop(0, MOE_BLOCK, wait, 0)

        gu = _dot(xbuf[...].astype(BF16), wgu_b[...]) + bgu_ref[0]
        gate = jnp.minimum(gu[:, :d_ff], SWIGLU_LIMIT)
        up = jnp.clip(gu[:, d_ff:], -SWIGLU_LIMIT, SWIGLU_LIMIT)
        act = (up + 1.0) * gate * (1.0 / (1.0 + jnp.exp(-SWIGLU_ALPHA * gate)))
        o_ref[...] = _dot(act.astype(BF16), wd_b[...]) + bd_ref[0]

    @pl.when(i >= nu_ref[0])
    def _():
        o_ref[...] = jnp.zeros_like(o_ref)


def _moe_experts(h, block_e, n_used, row_tok, w_gate_up, b_gate_up, w_down, b_down):
    n, d = h.shape
    n_exp, _, two_f = w_gate_up.shape
    d_ff = two_f // 2
    n_blocks = block_e.shape[0]
    grid_spec = pltpu.PrefetchScalarGridSpec(
        num_scalar_prefetch=2,
        grid=(n_blocks,),
        in_specs=[
            pl.BlockSpec((1, 1, MOE_BLOCK), lambda i, be, nu: (i, 0, 0), memory_space=pltpu.SMEM),
            pl.BlockSpec(memory_space=pl.ANY),
            pl.BlockSpec((1, d, two_f), lambda i, be, nu: (be[i], 0, 0)),
            pl.BlockSpec((1, 1, two_f), lambda i, be, nu: (be[i], 0, 0)),
            pl.BlockSpec((1, d_ff, d), lambda i, be, nu: (be[i], 0, 0)),
            pl.BlockSpec((1, 1, d), lambda i, be, nu: (be[i], 0, 0)),
        ],
        out_specs=pl.BlockSpec((MOE_BLOCK, d), lambda i, be, nu: (i, 0)),
        scratch_shapes=[
            pltpu.VMEM((MOE_BLOCK, d), F32),
            pltpu.VMEM((d, two_f), BF16),
            pltpu.VMEM((d_ff, d), BF16),
            pltpu.SemaphoreType.DMA,
        ],
    )
    return pl.pallas_call(
        _moe_kernel,
        grid_spec=grid_spec,
        out_shape=jax.ShapeDtypeStruct((n_blocks * MOE_BLOCK, d), F32),
        compiler_params=pltpu.CompilerParams(dimension_semantics=("arbitrary",), vmem_limit_bytes=VMEM_LIMIT),
        name="moe_experts",
    )(block_e, n_used, row_tok.reshape(n_blocks, 1, MOE_BLOCK), h, w_gate_up,
      b_gate_up.reshape(n_exp, 1, two_f), w_down, b_down.reshape(n_exp, 1, d))


def _combine_kernel(pos_ref, ys_hbm, x_ref, tw_ref, m_ref, gfin_ref, o_ref, buf, sem, *, final):
    tm = x_ref.shape[0]

    def row_copy(a):
        k = a // tm
        t = a - k * tm
        return pltpu.make_async_copy(ys_hbm.at[pl.ds(pos_ref[0, 0, a], 1)], buf.at[k, pl.ds(t, 1)], sem)

    def issue(a, carry):
        row_copy(a).start()
        return carry

    lax.fori_loop(0, TOP_K * tm, issue, 0)

    def wait(a, carry):
        row_copy(a).wait()
        return carry

    lax.fori_loop(0, TOP_K * tm, wait, 0)

    tw = tw_ref[...]
    moe = tw[:, 0:1] * buf[0]
    for k in range(1, TOP_K):
        moe = moe + tw[:, k:k + 1] * buf[k]
    x = x_ref[...] + m_ref[0][5:6] * moe
    if final:
        x = _rms(x, gfin_ref[...])
    o_ref[...] = x


def _moe_combine(ys, pos, x, tw, mods, g_final, seq, final):
    n, d = x.shape
    tm = min(COMBINE_TILE, seq)
    per_batch = seq // tm
    n_tiles = n // tm
    return pl.pallas_call(
        functools.partial(_combine_kernel, final=final),
        grid=(n_tiles,),
        in_specs=[
            pl.BlockSpec((1, 1, TOP_K * tm), lambda i: (i, 0, 0), memory_space=pltpu.SMEM),
            pl.BlockSpec(memory_space=pl.ANY),
            pl.BlockSpec((tm, d), lambda i: (i, 0)),
            pl.BlockSpec((tm, LANES), lambda i: (i, 0)),
            pl.BlockSpec((1, 6, d), lambda i: (i // per_batch, 0, 0)),
            pl.BlockSpec((1, d), lambda i: (0, 0)),
        ],
        out_specs=pl.BlockSpec((tm, d), lambda i: (i, 0)),
        out_shape=jax.ShapeDtypeStruct((n, d), F32),
        scratch_shapes=[pltpu.VMEM((TOP_K, tm, d), F32), pltpu.SemaphoreType.DMA],
        compiler_params=pltpu.CompilerParams(dimension_semantics=("arbitrary",), vmem_limit_bytes=VMEM_LIMIT),
        name="moe_combine",
    )(pos, ys, x, tw, mods, g_final.reshape(1, d))


def _dispatch_plan(top_e, n_tok):
    n_assign = n_tok * TOP_K
    flat_e = top_e.reshape(n_assign)
    order = jnp.argsort(flat_e).astype(jnp.int32)
    e_sorted = flat_e[order]
    counts = jnp.zeros((N_EXPERTS,), jnp.int32).at[flat_e].add(1)
    padded = (counts + MOE_BLOCK - 1) // MOE_BLOCK * MOE_BLOCK
    start = jnp.cumsum(counts) - counts
    pad_end = jnp.cumsum(padded)
    pad_start = pad_end - padded
    dest = pad_start[e_sorted] + jnp.arange(n_assign, dtype=jnp.int32) - start[e_sorted]
    n_blocks = -(-n_assign // MOE_BLOCK) + N_EXPERTS
    n_rows = n_blocks * MOE_BLOCK
    row_tok = jnp.zeros((n_rows,), jnp.int32).at[dest].set(order // TOP_K)
    pos = jnp.zeros((n_assign,), jnp.int32).at[order].set(dest)
    block_e = jnp.minimum(
        jnp.searchsorted(pad_end, jnp.arange(n_blocks, dtype=jnp.int32) * MOE_BLOCK, side="right"),
        N_EXPERTS - 1).astype(jnp.int32)
    n_used = (pad_end[-1] // MOE_BLOCK).astype(jnp.int32).reshape(1)
    return row_tok, pos, block_e, n_used


def _t5_bucket(dist):
    n = jnp.maximum(dist, 0)
    max_exact = NUM_BUCKETS // 2
    nf = jnp.maximum(n, 1).astype(F32)
    large = max_exact + (jnp.log(nf / max_exact) / math.log(MAX_DISTANCE / max_exact)
                         * (NUM_BUCKETS - max_exact)).astype(jnp.int32)
    large = jnp.minimum(large, NUM_BUCKETS - 1)
    return jnp.where(n < max_exact, n, large)


def _bias_tables(rel_bias):
    r = jnp.arange(Q_BLOCK)[:, None]
    j2 = jnp.arange(2 * Q_BLOCK)[None, :]
    swa = rel_bias[:, :SWA_HEADS][_t5_bucket(Q_BLOCK + r - j2)].transpose(2, 0, 1)
    tab = rel_bias[:, SWA_HEADS:]
    jb = jnp.arange(MOBA_BLOCK)[None, None, :]
    t_loc = (jnp.arange(MOBA_BLOCK // Q_BLOCK)[:, None, None] * Q_BLOCK + r[None])
    own = tab[_t5_bucket(t_loc - jb)].transpose(3, 0, 1, 2)
    prev = tab[_t5_bucket(t_loc + MOBA_BLOCK - jb)].transpose(3, 0, 1, 2)
    far = tab[_t5_bucket(jnp.int32(2 * MOBA_BLOCK))]
    return swa.astype(F32), own.astype(F32), prev.astype(F32), far.astype(F32)


def kernel(x, c, w_in, w_out, g_norm_mix, g_norm_ffn, g_group, w_mod, b_mod, swa_sinks, rel_bias,
           w_router, b_router, w_gate_up, b_gate_up, w_down, b_down, g_final):
    batch, seq, d = x.shape
    depth = w_in.shape[0]
    n = batch * seq
    assert seq % MOBA_BLOCK == 0 and seq // MOBA_BLOCK <= LANES and d % LANES == 0

    mods = _modulation(c, w_mod, b_mod)
    bias_swa, bias_own, bias_prev, bias_far = _bias_tables(rel_bias)
    w_in_b = w_in.astype(BF16)
    w_out_b = w_out.astype(BF16)
    w_router_b = jnp.zeros((depth, d, LANES), BF16).at[:, :, :N_EXPERTS].set(w_router.astype(BF16))
    b_router_p = jnp.full((depth, 1, LANES), NEG, F32).at[:, 0, :N_EXPERTS].set(b_router)

    xf = x.reshape(n, d)
    for l in range(depth):
        proj = _inproj(xf, g_norm_mix[l], mods[l], w_in_b[l], seq)
        o_sb = _sb_attention(proj, batch, seq)
        o_sw = _swa_attention(proj, swa_sinks[l], bias_swa, batch, seq)
        o_mb = _moba_attention(proj, bias_far, bias_own, bias_prev, batch, seq)
        xf, h, top_e, top_w = _outproj_router(o_sb, o_sw, o_mb, xf, mods[l], g_group[l], w_out_b[l],
                                              g_norm_ffn[l], w_router_b[l], b_router_p[l], seq)
        row_tok, pos, block_e, n_used = _dispatch_plan(top_e[:, :TOP_K], n)
        ys = _moe_experts(h, block_e, n_used, row_tok, w_gate_up[l], b_gate_up[l], w_down[l], b_down[l])
        tm = min(COMBINE_TILE, seq)
        pos_t = pos.reshape(n // tm, tm, TOP_K).transpose(0, 2, 1).reshape(n // tm, 1, TOP_K * tm)
        xf = _moe_combine(ys, pos_t, xf, top_w, mods[l], g_final, seq, final=(l == depth - 1))
    return xf.reshape(batch, seq, d)
```

```python
import functools
import math

import jax
import jax.numpy as jnp
from jax import lax
from jax.experimental import pallas as pl
from jax.experimental.pallas import tpu as pltpu

F32 = jnp.float32
BF16 = jnp.bfloat16

HEAD_DIM = 64
SB_HEADS = 4
SWA_HEADS = 8
SWA_KV_HEADS = 2
MOBA_HEADS = 4
SB_W = SB_HEADS * HEAD_DIM
SWA_W = SWA_HEADS * HEAD_DIM
SWA_KV_W = SWA_KV_HEADS * HEAD_DIM
MOBA_W = MOBA_HEADS * HEAD_DIM
Q_BLOCK = 128
WINDOW = 128
MOBA_BLOCK = 256
MOBA_TOPK = 3
NUM_BUCKETS = 32
MAX_DISTANCE = 128
N_EXPERTS = 32
TOP_K = 4
SWIGLU_LIMIT = 7.0
SWIGLU_ALPHA = 1.702
MOE_BLOCK = 256
EPS = 1e-6
NEG = -1e30
ATTN_SCALE = HEAD_DIM ** -0.5

LANES = 128
COL_SB_Q, COL_SB_K, COL_SB_V = 0, 2, 4
COL_SW_Q, COL_SW_K, COL_SW_V = 6, 10, 11
COL_MB_Q, COL_MB_K, COL_MB_V = 12, 14, 16
IN_W = 18 * LANES

SB_SPAN = 512
MOBA_GROUP = 4
TOKEN_TILE = 512
COMBINE_TILE = 128
VMEM_LIMIT = 56 * 1024 * 1024


def _rms(x, g):
    return x * lax.rsqrt(jnp.mean(x * x, axis=-1, keepdims=True) + EPS) * g


def _dot_t(a, b):
    return lax.dot_general(a, b, (((1,), (1,)), ((), ())), preferred_element_type=F32)


def _dot(a, b):
    return jnp.dot(a, b, preferred_element_type=F32)


def _mod_kernel(c_ref, w_ref, b_ref, o_ref):
    c = c_ref[...]
    ca = c * (1.0 / (1.0 + jnp.exp(-c)))
    o_ref[0] = _dot(ca, w_ref[0]) + b_ref[0]


def _modulation(c, w_mod, b_mod):
    depth, d, six_d = w_mod.shape
    b = c.shape[0]
    rows = 8
    c8 = jnp.zeros((rows, d), F32).at[:b].set(c)
    tn = six_d // 6
    out = pl.pallas_call(
        _mod_kernel,
        grid=(depth, six_d // tn),
        in_specs=[
            pl.BlockSpec((rows, d), lambda l, j: (0, 0)),
            pl.BlockSpec((1, d, tn), lambda l, j: (l, 0, j)),
            pl.BlockSpec((1, 1, tn), lambda l, j: (l, 0, j)),
        ],
        out_specs=pl.BlockSpec((1, rows, tn), lambda l, j: (l, 0, j)),
        out_shape=jax.ShapeDtypeStruct((depth, rows, six_d), F32),
        name="adaln_mod",
    )(c8, w_mod, b_mod.reshape(depth, 1, six_d))
    return out[:, :b].reshape(depth, b, 6, d)


def _inproj_kernel(x_ref, g_ref, m_ref, w_ref, o_ref):
    m = m_ref[0]
    h = _rms(x_ref[...], g_ref[...]) * (1.0 + m[1:2]) + m[0:1]
    hb = h.astype(BF16)
    step = 2 * LANES
    for j in range(IN_W // step):
        o_ref[:, j * step:(j + 1) * step] = _dot(hb, w_ref[:, j * step:(j + 1) * step]).astype(BF16)


def _inproj(x, g, mods, w_in_b, seq):
    n, d = x.shape
    tm = min(TOKEN_TILE, seq)
    per_batch = seq // tm
    return pl.pallas_call(
        _inproj_kernel,
        grid=(n // tm,),
        in_specs=[
            pl.BlockSpec((tm, d), lambda i: (i, 0)),
            pl.BlockSpec((1, d), lambda i: (0, 0)),
            pl.BlockSpec((1, 6, d), lambda i: (i // per_batch, 0, 0)),
            pl.BlockSpec((d, IN_W), lambda i: (0, 0)),
        ],
        out_specs=pl.BlockSpec((tm, IN_W), lambda i: (i, 0)),
        out_shape=jax.ShapeDtypeStruct((n, IN_W), BF16),
        compiler_params=pltpu.CompilerParams(dimension_semantics=("arbitrary",), vmem_limit_bytes=VMEM_LIMIT),
        name="norm_inproj",
    )(x, g.reshape(1, d), mods, w_in_b)


def _sb_kernel(q_ref, k_ref, v_ref, o_ref, *, span):
    i = pl.program_id(2)
    tq = q_ref.shape[0]
    n_chunks = span // LANES
    lane = lax.broadcasted_iota(jnp.int32, (tq, LANES), 1)
    tri_r = lax.broadcasted_iota(jnp.int32, (LANES, LANES), 0)
    tri_c = lax.broadcasted_iota(jnp.int32, (LANES, LANES), 1)
    tri = jnp.where(tri_r > tri_c, 1.0, 0.0).astype(BF16)
    scan_w = jnp.concatenate([tri, tri], axis=0)
    q = q_ref[...] * ATTN_SCALE
    qms = [jnp.where((lane >= HEAD_DIM * hh) & (lane < HEAD_DIM * (hh + 1)), q, jnp.zeros_like(q))
           for hh in range(2)]
    q_start = i * tq
    key_off = lax.broadcasted_iota(jnp.int32, (tq, span), 1) - lax.broadcasted_iota(jnp.int32, (tq, span), 0)

    def do_span(sidx, state, masked):
        start = pl.multiple_of(sidx * span, span)
        ks = k_ref[pl.ds(start, span), :]
        vs = v_ref[pl.ds(start, span), :]
        new_state = []
        for hh in range(2):
            run, acc = state[hh]
            z = _dot_t(qms[hh], ks)
            sp = jnp.maximum(z, 0.0) + jnp.log(1.0 + jnp.exp(-jnp.abs(z)))
            log_1m = -sp
            log_beta = z - sp
            parts = [None] * n_chunks
            for c in reversed(range(n_chunks)):
                cols = slice(c * LANES, (c + 1) * LANES)
                lc = log_1m[:, cols]
                if masked:
                    strict = (key_off[:, cols] + (start - q_start)) < 0
                    lc = jnp.where(strict, lc, 0.0)
                hi = lc.astype(BF16)
                lo = (lc - hi.astype(F32)).astype(BF16)
                later = _dot(jnp.concatenate([hi, lo], axis=1), scan_w) + run
                a = jnp.exp(log_beta[:, cols] + later)
                if masked:
                    a = jnp.where(strict, a, 0.0)
                parts[c] = a.astype(BF16)
                run = run + jnp.sum(lc, axis=-1, keepdims=True)
            acc = acc + _dot(jnp.concatenate(parts, axis=1), vs)
            new_state.append((run, acc))
        return tuple(new_state)

    init = ((jnp.zeros((tq, 1), F32), jnp.zeros((tq, LANES), F32)),) * 2
    top = q_start // span
    state = do_span(top, init, True)
    state = lax.fori_loop(0, top, lambda jj, st: do_span(top - 1 - jj, st, False), state)
    o_ref[...] = jnp.where(lane < HEAD_DIM, state[0][1], state[1][1])


def _sb_attention(proj, batch, seq):
    n = proj.shape[0]
    nq = seq // Q_BLOCK
    pairs = SB_HEADS // 2
    return pl.pallas_call(
        functools.partial(_sb_kernel, span=min(SB_SPAN, seq)),
        grid=(batch, pairs, nq),
        in_specs=[
            pl.BlockSpec((Q_BLOCK, LANES), lambda b, p, i: (b * nq + i, COL_SB_Q + p)),
            pl.BlockSpec((seq, LANES), lambda b, p, i: (b, COL_SB_K + p)),
            pl.BlockSpec((seq, LANES), lambda b, p, i: (b, COL_SB_V + p)),
        ],
        out_specs=pl.BlockSpec((Q_BLOCK, LANES), lambda b, p, i: (b * nq + i, p)),
        out_shape=jax.ShapeDtypeStruct((n, SB_W), F32),
        compiler_params=pltpu.CompilerParams(
            dimension_semantics=("arbitrary", "arbitrary", "arbitrary"), vmem_limit_bytes=VMEM_LIMIT),
        name="sb_attention",
    )(proj, proj, proj)


def _swa_kernel(sink_ref, qa_ref, qb_ref, kp_ref, kc_ref, vp_ref, vc_ref, bias_ref, o_ref):
    i = pl.program_id(1)
    kk = jnp.concatenate([kp_ref[...], kc_ref[...]], axis=0)
    vv = jnp.concatenate([vp_ref[...], vc_ref[...]], axis=0)
    r = lax.broadcasted_iota(jnp.int32, (Q_BLOCK, 2 * Q_BLOCK), 0)
    j = lax.broadcasted_iota(jnp.int32, (Q_BLOCK, 2 * Q_BLOCK), 1)
    dist = Q_BLOCK + r - j
    valid = (dist >= 0) & (dist < WINDOW) & ((j >= Q_BLOCK) | (i > 0))
    group = SWA_HEADS // SWA_KV_HEADS
    for h in range(SWA_HEADS):
        g = h // group
        q_ref = qa_ref if h < group else qb_ref
        c0 = HEAD_DIM * (h % group)
        qh = q_ref[:, c0:c0 + HEAD_DIM] * ATTN_SCALE
        s = _dot_t(qh, kk[:, HEAD_DIM * g:HEAD_DIM * (g + 1)]) + bias_ref[h]
        s = jnp.where(valid, s, NEG)
        sink = sink_ref[h]
        m = jnp.maximum(jnp.max(s, axis=-1, keepdims=True), sink)
        p = jnp.exp(s - m)
        l = jnp.sum(p, axis=-1, keepdims=True) + jnp.exp(sink - m)
        o = _dot(p.astype(BF16), vv[:, HEAD_DIM * g:HEAD_DIM * (g + 1)])
        o_ref[:, HEAD_DIM * h:HEAD_DIM * (h + 1)] = o / l


def _swa_attention(proj, sinks, bias, batch, seq):
    n = proj.shape[0]
    nq = seq // Q_BLOCK
    wide = 2 * LANES
    grid_spec = pltpu.PrefetchScalarGridSpec(
        num_scalar_prefetch=1,
        grid=(batch, nq),
        in_specs=[
            pl.BlockSpec((Q_BLOCK, wide), lambda b, i, s: (b * nq + i, COL_SW_Q // 2)),
            pl.BlockSpec((Q_BLOCK, wide), lambda b, i, s: (b * nq + i, COL_SW_Q // 2 + 1)),
            pl.BlockSpec((Q_BLOCK, LANES), lambda b, i, s: (b * nq + jnp.maximum(i - 1, 0), COL_SW_K)),
            pl.BlockSpec((Q_BLOCK, LANES), lambda b, i, s: (b * nq + i, COL_SW_K)),
            pl.BlockSpec((Q_BLOCK, LANES), lambda b, i, s: (b * nq + jnp.maximum(i - 1, 0), COL_SW_V)),
            pl.BlockSpec((Q_BLOCK, LANES), lambda b, i, s: (b * nq + i, COL_SW_V)),
            pl.BlockSpec((SWA_HEADS, Q_BLOCK, 2 * Q_BLOCK), lambda b, i, s: (0, 0, 0)),
        ],
        out_specs=pl.BlockSpec((Q_BLOCK, SWA_W), lambda b, i, s: (b * nq + i, 0)),
    )
    return pl.pallas_call(
        _swa_kernel,
        grid_spec=grid_spec,
        out_shape=jax.ShapeDtypeStruct((n, SWA_W), F32),
        compiler_params=pltpu.CompilerParams(dimension_semantics=("arbitrary", "arbitrary")),
        name="swa_attention",
    )(sinks, proj, proj, proj, proj, proj, proj, bias)


def _moba_kernel(far_ref, q_ref, k_ref, v_ref, btop_ref, o_ref, kmean_ref, *, group):
    p = pl.program_id(1)
    i = pl.program_id(2)
    seq = k_ref.shape[0]
    tq = q_ref.shape[0]
    nb = seq // MOBA_BLOCK
    top_w = 2 * MOBA_BLOCK
    far_w = group * MOBA_BLOCK
    lane = lax.broadcasted_iota(jnp.int32, (tq, LANES), 1)

    @pl.when(i == 0)
    def _():
        blk = lax.broadcasted_iota(jnp.int32, (LANES, seq), 0)
        pos = lax.broadcasted_iota(jnp.int32, (LANES, seq), 1)
        lo = blk * MOBA_BLOCK
        avg = jnp.where((pos >= lo) & (pos < lo + MOBA_BLOCK), 1.0 / MOBA_BLOCK, 0.0).astype(BF16)
        kmean_ref[...] = _dot(avg, k_ref[...]).astype(BF16)

    cur = (i * Q_BLOCK) // MOBA_BLOCK
    first_row = (i * Q_BLOCK) % MOBA_BLOCK
    q = q_ref[...]
    kmean = kmean_ref[...]

    prev_start = pl.multiple_of(jnp.maximum(cur - 1, 0) * MOBA_BLOCK, MOBA_BLOCK)
    own_start = pl.multiple_of(cur * MOBA_BLOCK, MOBA_BLOCK)
    k_top = jnp.concatenate([k_ref[pl.ds(prev_start, MOBA_BLOCK), :], k_ref[pl.ds(own_start, MOBA_BLOCK), :]], axis=0)
    v_top = jnp.concatenate([v_ref[pl.ds(prev_start, MOBA_BLOCK), :], v_ref[pl.ds(own_start, MOBA_BLOCK), :]], axis=0)
    jt = lax.broadcasted_iota(jnp.int32, (tq, top_w), 1)
    rt = lax.broadcasted_iota(jnp.int32, (tq, top_w), 0)
    prev_half = jt < MOBA_BLOCK
    own_causal = (jt >= MOBA_BLOCK) & ((first_row + rt) >= (jt - MOBA_BLOCK))
    pick_row = lax.broadcasted_iota(jnp.int32, (LANES, LANES), 0)
    far_row = lax.broadcasted_iota(jnp.int32, (LANES, far_w), 0)
    far_blk = lax.broadcasted_iota(jnp.int32, (LANES, far_w), 1) // MOBA_BLOCK

    def update(state, s, vb):
        m, l, acc = state
        m_new = jnp.maximum(m, jnp.max(s, axis=-1, keepdims=True))
        alpha = jnp.exp(m - m_new)
        pexp = jnp.exp(s - m_new)
        l = alpha * l + jnp.sum(pexp, axis=-1, keepdims=True)
        acc = alpha * acc + _dot(pexp.astype(BF16), vb)
        return m_new, l, acc

    qms, sel_far, states = [], [], []
    for hh in range(2):
        hmask = (lane >= HEAD_DIM * hh) & (lane < HEAD_DIM * (hh + 1))
        qg = jnp.where(hmask, q, jnp.zeros_like(q))
        qm = qg * ATTN_SCALE
        gate = jnp.where(lane < cur, _dot_t(qg, kmean), NEG)
        sel = jnp.zeros((tq, LANES), jnp.bool_)
        for _ in range(min(MOBA_TOPK, nb)):
            mx = jnp.max(gate, axis=-1, keepdims=True)
            first = jnp.min(jnp.where(gate == mx, lane, LANES), axis=-1, keepdims=True)
            hit = lane == first
            sel = sel | hit
            gate = jnp.where(hit, -jnp.inf, gate)
        sel_prev = jnp.where(sel & (lane == cur - 1), 1.0, 0.0).astype(BF16)
        rep = _dot(sel_prev, jnp.where(pick_row == cur - 1, 1.0, 0.0).astype(BF16))
        prev_ok = jnp.concatenate([rep] * (top_w // LANES), axis=1) > 0.5
        s = _dot_t(qm, k_top) + btop_ref[hh, 0]
        s = jnp.where(own_causal | (prev_half & prev_ok), s, NEG)
        m = jnp.max(s, axis=-1, keepdims=True)
        pexp = jnp.exp(s - m)
        l = jnp.sum(pexp, axis=-1, keepdims=True)
        acc = _dot(pexp.astype(BF16), v_top)
        qms.append(qm)
        sel_far.append(jnp.where(sel & (lane < cur - 1), 1.0, 0.0).astype(BF16))
        states.append((m, l, acc))

    def far_group(g, states):
        start = pl.multiple_of(g * far_w, far_w)
        kg = k_ref[pl.ds(start, far_w), :]
        vg = v_ref[pl.ds(start, far_w), :]
        pick = jnp.where(far_row == g * group + far_blk, 1.0, 0.0).astype(BF16)
        out = []
        for hh in range(2):
            s = _dot_t(qms[hh], kg) + far_ref[2 * p + hh]
            s = jnp.where(_dot(sel_far[hh], pick) > 0.5, s, NEG)
            out.append(update(states[hh], s, vg))
        return tuple(out)

    n_far = jnp.maximum(cur - 1, 0)
    states = lax.fori_loop(0, (n_far + group - 1) // group, far_group, tuple(states))
    outs = [acc / l for (_, l, acc) in states]
    o_ref[...] = jnp.where(lane < HEAD_DIM, outs[0], outs[1])


def _moba_attention(proj, far_bias, bias_top, batch, seq):
    n = proj.shape[0]
    nq = seq // Q_BLOCK
    pairs = MOBA_HEADS // 2
    per_blk = MOBA_BLOCK // Q_BLOCK
    nb = seq // MOBA_BLOCK
    group = min(MOBA_GROUP, nb)
    assert nb % group == 0
    grid_spec = pltpu.PrefetchScalarGridSpec(
        num_scalar_prefetch=1,
        grid=(batch, pairs, nq),
        in_specs=[
            pl.BlockSpec((Q_BLOCK, LANES), lambda b, p, i, f: (b * nq + i, COL_MB_Q + p)),
            pl.BlockSpec((seq, LANES), lambda b, p, i, f: (b, COL_MB_K + p)),
            pl.BlockSpec((seq, LANES), lambda b, p, i, f: (b, COL_MB_V + p)),
            pl.BlockSpec((2, 1, Q_BLOCK, 2 * MOBA_BLOCK), lambda b, p, i, f: (p, i % per_blk, 0, 0)),
        ],
        out_specs=pl.BlockSpec((Q_BLOCK, LANES), lambda b, p, i, f: (b * nq + i, p)),
        scratch_shapes=[pltpu.VMEM((LANES, LANES), BF16)],
    )
    return pl.pallas_call(
        functools.partial(_moba_kernel, group=group),
        grid_spec=grid_spec,
        out_shape=jax.ShapeDtypeStruct((n, MOBA_W), F32),
        compiler_params=pltpu.CompilerParams(
            dimension_semantics=("arbitrary", "arbitrary", "arbitrary"), vmem_limit_bytes=VMEM_LIMIT),
        name="moba_attention",
    )(far_bias, proj, proj, proj, bias_top)


def _outproj_kernel(osb_ref, osw_ref, omb_ref, x_ref, m_ref, gg_ref, wo_ref, gf_ref, wr_ref, br_ref,
                    xo_ref, h_ref, te_ref, tw_ref):
    m = m_ref[0]
    c1, c2 = SB_W, SB_W + SWA_W
    y = _dot(_rms(osb_ref[...], gg_ref[:, :c1]).astype(BF16), wo_ref[:c1, :])
    y = y + _dot(_rms(osw_ref[...], gg_ref[:, c1:c2]).astype(BF16), wo_ref[c1:c2, :])
    y = y + _dot(_rms(omb_ref[...], gg_ref[:, c2:]).astype(BF16), wo_ref[c2:, :])
    x = x_ref[...] + m[2:3] * y
    xo_ref[...] = x
    h = _rms(x, gf_ref[...]) * (1.0 + m[4:5]) + m[3:4]
    h_ref[...] = h
    logits = _dot(h.astype(BF16), wr_ref[...]) + br_ref[...]
    lane = lax.broadcasted_iota(jnp.int32, logits.shape, 1)
    ids = jnp.zeros(logits.shape, jnp.int32)
    wts = jnp.zeros(logits.shape, F32)
    top = None
    denom = None
    for r in range(TOP_K):
        mx = jnp.max(logits, axis=-1, keepdims=True)
        first = jnp.min(jnp.where(logits == mx, lane, LANES), axis=-1, keepdims=True)
        logits = jnp.where(lane == first, -jnp.inf, logits)
        if r == 0:
            top = mx
        e = jnp.exp(mx - top)
        denom = e if r == 0 else denom + e
        ids = jnp.where(lane == r, first, ids)
        wts = jnp.where(lane == r, e, wts)
    te_ref[...] = ids
    tw_ref[...] = wts / denom


def _outproj_router(o_sb, o_sw, o_mb, x, mods, g_group, w_out_b, g_ffn, w_router_b, b_router_p, seq):
    n, d = x.shape
    tm = min(TOKEN_TILE, seq)
    per_batch = seq // tm
    row = lambda i: (i, 0)
    const = lambda i: (0, 0)
    return pl.pallas_call(
        _outproj_kernel,
        grid=(n // tm,),
        in_specs=[
            pl.BlockSpec((tm, SB_W), row),
            pl.BlockSpec((tm, SWA_W), row),
            pl.BlockSpec((tm, MOBA_W), row),
            pl.BlockSpec((tm, d), row),
            pl.BlockSpec((1, 6, d), lambda i: (i // per_batch, 0, 0)),
            pl.BlockSpec((1, d), const),
            pl.BlockSpec((d, d), const),
            pl.BlockSpec((1, d), const),
            pl.BlockSpec((d, LANES), const),
            pl.BlockSpec((1, LANES), const),
        ],
        out_specs=[
            pl.BlockSpec((tm, d), row),
            pl.BlockSpec((tm, d), row),
            pl.BlockSpec((tm, LANES), row),
            pl.BlockSpec((tm, LANES), row),
        ],
        out_shape=[
            jax.ShapeDtypeStruct((n, d), F32),
            jax.ShapeDtypeStruct((n, d), F32),
            jax.ShapeDtypeStruct((n, LANES), jnp.int32),
            jax.ShapeDtypeStruct((n, LANES), F32),
        ],
        compiler_params=pltpu.CompilerParams(dimension_semantics=("arbitrary",), vmem_limit_bytes=VMEM_LIMIT),
        name="outproj_router",
    )(o_sb, o_sw, o_mb, x, mods, g_group.reshape(1, d), w_out_b, g_ffn.reshape(1, d), w_router_b, b_router_p)


def _moe_kernel(be_ref, nu_ref, tok_ref, h_hbm, wgu_ref, bgu_ref, wd_ref, bd_ref, o_ref,
                xbuf, wgu_b, wd_b, sem):
    i = pl.program_id(0)
    d_ff = wd_ref.shape[1]

    def row_copy(r):
        return pltpu.make_async_copy(h_hbm.at[pl.ds(tok_ref[0, 0, r], 1)], xbuf.at[pl.ds(r, 1)], sem)

    @pl.when(i < nu_ref[0])
    def _():
        def issue(r, carry):
            row_copy(r).start()
            return carry

        lax.fori_loop(0, MOE_BLOCK, issue, 0)

        changed = jnp.logical_or(i == 0, be_ref[i] != be_ref[jnp.maximum(i - 1, 0)])

        @pl.when(changed)
        def _():
            wgu_b[...] = wgu_ref[0].astype(BF16)
            wd_b[...] = wd_ref[0].astype(BF16)

        def wait(r, carry):
            row_copy(r).wait()
            return carry

        lax.fori_loop(0, MOE_BLOCK, wait, 0)

        gu = _dot(xbuf[...].astype(BF16), wgu_b[...]) + bgu_ref[0]
        gate = jnp.minimum(gu[:, :d_ff], SWIGLU_LIMIT)
        up = jnp.clip(gu[:, d_ff:], -SWIGLU_LIMIT, SWIGLU_LIMIT)
        act = (up + 1.0) * gate * (1.0 / (1.0 + jnp.exp(-SWIGLU_ALPHA * gate)))
        o_ref[...] = _dot(act.astype(BF16), wd_b[...]) + bd_ref[0]

    @pl.when(i >= nu_ref[0])
    def _():
        o_ref[...] = jnp.zeros_like(o_ref)


def _moe_experts(h, block_e, n_used, row_tok, w_gate_up, b_gate_up, w_down, b_down):
    n, d = h.shape
    n_exp, _, two_f = w_gate_up.shape
    d_ff = two_f // 2
    n_blocks = block_e.shape[0]
    grid_spec = pltpu.PrefetchScalarGridSpec(
        num_scalar_prefetch=2,
        grid=(n_blocks,),
        in_specs=[
            pl.BlockSpec((1, 1, MOE_BLOCK), lambda i, be, nu: (i, 0, 0), memory_space=pltpu.SMEM),
            pl.BlockSpec(memory_space=pl.ANY),
            pl.BlockSpec((1, d, two_f), lambda i, be, nu: (be[i], 0, 0)),
            pl.BlockSpec((1, 1, two_f), lambda i, be, nu: (be[i], 0, 0)),
            pl.BlockSpec((1, d_ff, d), lambda i, be, nu: (be[i], 0, 0)),
            pl.BlockSpec((1, 1, d), lambda i, be, nu: (be[i], 0, 0)),
        ],
        out_specs=pl.BlockSpec((MOE_BLOCK, d), lambda i, be, nu: (i, 0)),
        scratch_shapes=[
            pltpu.VMEM((MOE_BLOCK, d), F32),
            pltpu.VMEM((d, two_f), BF16),
            pltpu.VMEM((d_ff, d), BF16),
            pltpu.SemaphoreType.DMA,
        ],
    )
    return pl.pallas_call(
        _moe_kernel,
        grid_spec=grid_spec,
        out_shape=jax.ShapeDtypeStruct((n_blocks * MOE_BLOCK, d), F32),
        compiler_params=pltpu.CompilerParams(dimension_semantics=("arbitrary",), vmem_limit_bytes=VMEM_LIMIT),
        name="moe_experts",
    )(block_e, n_used, row_tok.reshape(n_blocks, 1, MOE_BLOCK), h, w_gate_up,
      b_gate_up.reshape(n_exp, 1, two_f), w_down, b_down.reshape(n_exp, 1, d))


def _combine_kernel(pos_ref, ys_hbm, x_ref, tw_ref, m_ref, gfin_ref, o_ref, buf, sem, *, final):
    tm = x_ref.shape[0]

    def row_copy(a):
        k = a // tm
        t = a - k * tm
        return pltpu.make_async_copy(ys_hbm.at[pl.ds(pos_ref[0, 0, a], 1)], buf.at[k, pl.ds(t, 1)], sem)

    def issue(a, carry):
        row_copy(a).start()
        return carry

    lax.fori_loop(0, TOP_K * tm, issue, 0)

    def wait(a, carry):
        row_copy(a).wait()
        return carry

    lax.fori_loop(0, TOP_K * tm, wait, 0)

    tw = tw_ref[...]
    moe = tw[:, 0:1] * buf[0]
    for k in range(1, TOP_K):
        moe = moe + tw[:, k:k + 1] * buf[k]
    x = x_ref[...] + m_ref[0][5:6] * moe
    if final:
        x = _rms(x, gfin_ref[...])
    o_ref[...] = x


def _moe_combine(ys, pos, x, tw, mods, g_final, seq, final):
    n, d = x.shape
    tm = min(COMBINE_TILE, seq)
    per_batch = seq // tm
    n_tiles = n // tm
    return pl.pallas_call(
        functools.partial(_combine_kernel, final=final),
        grid=(n_tiles,),
        in_specs=[
            pl.BlockSpec((1, 1, TOP_K * tm), lambda i: (i, 0, 0), memory_space=pltpu.SMEM),
            pl.BlockSpec(memory_space=pl.ANY),
            pl.BlockSpec((tm, d), lambda i: (i, 0)),
            pl.BlockSpec((tm, LANES), lambda i: (i, 0)),
            pl.BlockSpec((1, 6, d), lambda i: (i // per_batch, 0, 0)),
            pl.BlockSpec((1, d), lambda i: (0, 0)),
        ],
        out_specs=pl.BlockSpec((tm, d), lambda i: (i, 0)),
        out_shape=jax.ShapeDtypeStruct((n, d), F32),
        scratch_shapes=[pltpu.VMEM((TOP_K, tm, d), F32), pltpu.SemaphoreType.DMA],
        compiler_params=pltpu.CompilerParams(dimension_semantics=("arbitrary",), vmem_limit_bytes=VMEM_LIMIT),
        name="moe_combine",
    )(pos, ys, x, tw, mods, g_final.reshape(1, d))


def _dispatch_plan(top_e, n_tok):
    n_assign = n_tok * TOP_K
    flat_e = top_e.reshape(n_assign)
    order = jnp.argsort(flat_e).astype(jnp.int32)
    e_sorted = flat_e[order]
    counts = jnp.zeros((N_EXPERTS,), jnp.int32).at[flat_e].add(1)
    padded = (counts + MOE_BLOCK - 1) // MOE_BLOCK * MOE_BLOCK
    start = jnp.cumsum(counts) - counts
    pad_end = jnp.cumsum(padded)
    pad_start = pad_end - padded
    dest = pad_start[e_sorted] + jnp.arange(n_assign, dtype=jnp.int32) - start[e_sorted]
    n_blocks = -(-n_assign // MOE_BLOCK) + N_EXPERTS
    n_rows = n_blocks * MOE_BLOCK
    row_tok = jnp.zeros((n_rows,), jnp.int32).at[dest].set(order // TOP_K)
    pos = jnp.zeros((n_assign,), jnp.int32).at[order].set(dest)
    block_e = jnp.minimum(
        jnp.searchsorted(pad_end, jnp.arange(n_blocks, dtype=jnp.int32) * MOE_BLOCK, side="right"),
        N_EXPERTS - 1).astype(jnp.int32)
    n_used = (pad_end[-1] // MOE_BLOCK).astype(jnp.int32).reshape(1)
    return row_tok, pos, block_e, n_used


def _t5_bucket(dist):
    n = jnp.maximum(dist, 0)
    max_exact = NUM_BUCKETS // 2
    nf = jnp.maximum(n, 1).astype(F32)
    large = max_exact + (jnp.log(nf / max_exact) / math.log(MAX_DISTANCE / max_exact)
                         * (NUM_BUCKETS - max_exact)).astype(jnp.int32)
    large = jnp.minimum(large, NUM_BUCKETS - 1)
    return jnp.where(n < max_exact, n, large)


def _bias_tables(rel_bias):
    r = jnp.arange(Q_BLOCK)[:, None]
    j2 = jnp.arange(2 * Q_BLOCK)[None, :]
    swa = rel_bias[:, :SWA_HEADS][_t5_bucket(Q_BLOCK + r - j2)].transpose(2, 0, 1)
    tab = rel_bias[:, SWA_HEADS:]
    jb = jnp.arange(MOBA_BLOCK)[None, None, :]
    t_loc = (jnp.arange(MOBA_BLOCK // Q_BLOCK)[:, None, None] * Q_BLOCK + r[None])
    own = tab[_t5_bucket(t_loc - jb)].transpose(3, 0, 1, 2)
    prev = tab[_t5_bucket(t_loc + MOBA_BLOCK - jb)].transpose(3, 0, 1, 2)
    far = tab[_t5_bucket(jnp.int32(2 * MOBA_BLOCK))]
    top = jnp.concatenate([prev, own], axis=-1)
    return swa.astype(F32), top.astype(F32), far.astype(F32)


def kernel(x, c, w_in, w_out, g_norm_mix, g_norm_ffn, g_group, w_mod, b_mod, swa_sinks, rel_bias,
           w_router, b_router, w_gate_up, b_gate_up, w_down, b_down, g_final):
    batch, seq, d = x.shape
    depth = w_in.shape[0]
    n = batch * seq
    assert seq % MOBA_BLOCK == 0 and seq // MOBA_BLOCK <= LANES and d % LANES == 0

    mods = _modulation(c, w_mod, b_mod)
    bias_swa, bias_top, bias_far = _bias_tables(rel_bias)
    w_in_b = w_in.astype(BF16)
    w_out_b = w_out.astype(BF16)
    w_router_b = jnp.zeros((depth, d, LANES), BF16).at[:, :, :N_EXPERTS].set(w_router.astype(BF16))
    b_router_p = jnp.full((depth, 1, LANES), NEG, F32).at[:, 0, :N_EXPERTS].set(b_router)

    xf = x.reshape(n, d)
    for l in range(depth):
        proj = _inproj(xf, g_norm_mix[l], mods[l], w_in_b[l], seq)
        o_sb = _sb_attention(proj, batch, seq)
        o_sw = _swa_attention(proj, swa_sinks[l], bias_swa, batch, seq)
        o_mb = _moba_attention(proj, bias_far, bias_top, batch, seq)
        xf, h, top_e, top_w = _outproj_router(o_sb, o_sw, o_mb, xf, mods[l], g_group[l], w_out_b[l],
                                              g_norm_ffn[l], w_router_b[l], b_router_p[l], seq)
        row_tok, pos, block_e, n_used = _dispatch_plan(top_e[:, :TOP_K], n)
        ys = _moe_experts(h, block_e, n_used, row_tok, w_gate_up[l], b_gate_up[l], w_down[l], b_down[l])
        tm = min(COMBINE_TILE, seq)
        pos_t = pos.reshape(n // tm, tm, TOP_K).transpose(0, 2, 1).reshape(n // tm, 1, TOP_K * tm)
        xf = _moe_combine(ys, pos_t, xf, top_w, mods[l], g_final, seq, final=(l == depth - 1))
    return xf.reshape(batch, seq, d)
```

```python
import functools
import math

import jax
import jax.numpy as jnp
from jax import lax
from jax.experimental import pallas as pl
from jax.experimental.pallas import tpu as pltpu

F32 = jnp.float32
BF16 = jnp.bfloat16

HEAD_DIM = 64
SB_HEADS = 4
SWA_HEADS = 8
SWA_KV_HEADS = 2
MOBA_HEADS = 4
SB_W = SB_HEADS * HEAD_DIM
SWA_W = SWA_HEADS * HEAD_DIM
SWA_KV_W = SWA_KV_HEADS * HEAD_DIM
MOBA_W = MOBA_HEADS * HEAD_DIM
Q_BLOCK = 128
WINDOW = 128
MOBA_BLOCK = 256
MOBA_TOPK = 3
NUM_BUCKETS = 32
MAX_DISTANCE = 128
N_EXPERTS = 32
TOP_K = 4
SWIGLU_LIMIT = 7.0
SWIGLU_ALPHA = 1.702
MOE_BLOCK = 256
EPS = 1e-6
NEG = -1e30
ATTN_SCALE = HEAD_DIM ** -0.5

LANES = 128
COL_SB_Q, COL_SB_K, COL_SB_V = 0, 2, 4
COL_SW_Q, COL_SW_K, COL_SW_V = 6, 10, 11
COL_MB_Q, COL_MB_K, COL_MB_V = 12, 14, 16
IN_W = 18 * LANES

SB_SPAN = 512
MOBA_GROUP = 4
TOKEN_TILE = 512
MOE_ROWS = 512
DISPATCH_TILE = 256
COMBINE_TILE = 256
VMEM_LIMIT = 56 * 1024 * 1024


def _rms(x, g):
    return x * lax.rsqrt(jnp.mean(x * x, axis=-1, keepdims=True) + EPS) * g


def _dot_t(a, b):
    return lax.dot_general(a, b, (((1,), (1,)), ((), ())), preferred_element_type=F32)


def _dot(a, b):
    return jnp.dot(a, b, preferred_element_type=F32)


def _mod_kernel(c_ref, w_ref, b_ref, o_ref):
    c = c_ref[...]
    ca = c * (1.0 / (1.0 + jnp.exp(-c)))
    o_ref[0] = _dot(ca, w_ref[0]) + b_ref[0]


def _modulation(c, w_mod, b_mod):
    depth, d, six_d = w_mod.shape
    b = c.shape[0]
    rows = 8
    c8 = jnp.zeros((rows, d), F32).at[:b].set(c)
    tn = six_d // 6
    out = pl.pallas_call(
        _mod_kernel,
        grid=(depth, six_d // tn),
        in_specs=[
            pl.BlockSpec((rows, d), lambda l, j: (0, 0)),
            pl.BlockSpec((1, d, tn), lambda l, j: (l, 0, j)),
            pl.BlockSpec((1, 1, tn), lambda l, j: (l, 0, j)),
        ],
        out_specs=pl.BlockSpec((1, rows, tn), lambda l, j: (l, 0, j)),
        out_shape=jax.ShapeDtypeStruct((depth, rows, six_d), F32),
        name="adaln_mod",
    )(c8, w_mod, b_mod.reshape(depth, 1, six_d))
    return out[:, :b].reshape(depth, b, 6, d)


def _inproj_kernel(x_ref, g_ref, m_ref, w_ref, o_ref):
    m = m_ref[0]
    h = _rms(x_ref[...], g_ref[...]) * (1.0 + m[1:2]) + m[0:1]
    hb = h.astype(BF16)
    step = 2 * LANES
    for j in range(IN_W // step):
        o_ref[:, j * step:(j + 1) * step] = _dot(hb, w_ref[:, j * step:(j + 1) * step]).astype(BF16)


def _inproj(x, g, mods, w_in_b, seq):
    n, d = x.shape
    tm = min(TOKEN_TILE, seq)
    per_batch = seq // tm
    return pl.pallas_call(
        _inproj_kernel,
        grid=(n // tm,),
        in_specs=[
            pl.BlockSpec((tm, d), lambda i: (i, 0)),
            pl.BlockSpec((1, d), lambda i: (0, 0)),
            pl.BlockSpec((1, 6, d), lambda i: (i // per_batch, 0, 0)),
            pl.BlockSpec((d, IN_W), lambda i: (0, 0)),
        ],
        out_specs=pl.BlockSpec((tm, IN_W), lambda i: (i, 0)),
        out_shape=jax.ShapeDtypeStruct((n, IN_W), BF16),
        compiler_params=pltpu.CompilerParams(dimension_semantics=("arbitrary",), vmem_limit_bytes=VMEM_LIMIT),
        name="norm_inproj",
    )(x, g.reshape(1, d), mods, w_in_b)


def _sb_kernel(q_ref, k_ref, v_ref, o_ref, *, span):
    i = pl.program_id(2)
    tq = q_ref.shape[0]
    n_chunks = span // LANES
    lane = lax.broadcasted_iota(jnp.int32, (tq, LANES), 1)
    tri_r = lax.broadcasted_iota(jnp.int32, (LANES, LANES), 0)
    tri_c = lax.broadcasted_iota(jnp.int32, (LANES, LANES), 1)
    tri = jnp.where(tri_r > tri_c, 1.0, 0.0).astype(BF16)
    scan_w = jnp.concatenate([tri, tri], axis=0)
    q = q_ref[...] * ATTN_SCALE
    qms = [jnp.where((lane >= HEAD_DIM * hh) & (lane < HEAD_DIM * (hh + 1)), q, jnp.zeros_like(q))
           for hh in range(2)]
    q_start = i * tq
    key_off = lax.broadcasted_iota(jnp.int32, (tq, span), 1) - lax.broadcasted_iota(jnp.int32, (tq, span), 0)

    def do_span(sidx, state, masked):
        start = pl.multiple_of(sidx * span, span)
        ks = k_ref[pl.ds(start, span), :]
        vs = v_ref[pl.ds(start, span), :]
        new_state = []
        for hh in range(2):
            run, acc = state[hh]
            z = _dot_t(qms[hh], ks)
            sp = jnp.maximum(z, 0.0) + jnp.log(1.0 + jnp.exp(-jnp.abs(z)))
            log_1m = -sp
            log_beta = z - sp
            parts = [None] * n_chunks
            for c in reversed(range(n_chunks)):
                cols = slice(c * LANES, (c + 1) * LANES)
                lc = log_1m[:, cols]
                if masked:
                    strict = (key_off[:, cols] + (start - q_start)) < 0
                    lc = jnp.where(strict, lc, 0.0)
                hi = lc.astype(BF16)
                lo = (lc - hi.astype(F32)).astype(BF16)
                later = _dot(jnp.concatenate([hi, lo], axis=1), scan_w) + run
                a = jnp.exp(log_beta[:, cols] + later)
                if masked:
                    a = jnp.where(strict, a, 0.0)
                parts[c] = a.astype(BF16)
                run = run + jnp.sum(lc, axis=-1, keepdims=True)
            acc = acc + _dot(jnp.concatenate(parts, axis=1), vs)
            new_state.append((run, acc))
        return tuple(new_state)

    init = ((jnp.zeros((tq, 1), F32), jnp.zeros((tq, LANES), F32)),) * 2
    top = q_start // span
    state = do_span(top, init, True)
    state = lax.fori_loop(0, top, lambda jj, st: do_span(top - 1 - jj, st, False), state)
    o_ref[...] = jnp.where(lane < HEAD_DIM, state[0][1], state[1][1])


def _sb_attention(proj, batch, seq):
    n = proj.shape[0]
    nq = seq // Q_BLOCK
    pairs = SB_HEADS // 2
    return pl.pallas_call(
        functools.partial(_sb_kernel, span=min(SB_SPAN, seq)),
        grid=(batch, pairs, nq),
        in_specs=[
            pl.BlockSpec((Q_BLOCK, LANES), lambda b, p, i: (b * nq + i, COL_SB_Q + p)),
            pl.BlockSpec((seq, LANES), lambda b, p, i: (b, COL_SB_K + p)),
            pl.BlockSpec((seq, LANES), lambda b, p, i: (b, COL_SB_V + p)),
        ],
        out_specs=pl.BlockSpec((Q_BLOCK, LANES), lambda b, p, i: (b * nq + i, p)),
        out_shape=jax.ShapeDtypeStruct((n, SB_W), F32),
        compiler_params=pltpu.CompilerParams(
            dimension_semantics=("arbitrary", "arbitrary", "arbitrary"), vmem_limit_bytes=VMEM_LIMIT),
        name="sb_attention",
    )(proj, proj, proj)


def _swa_kernel(sink_ref, qa_ref, qb_ref, kp_ref, kc_ref, vp_ref, vc_ref, bias_ref, o_ref):
    i = pl.program_id(1)
    kk = jnp.concatenate([kp_ref[...], kc_ref[...]], axis=0)
    vv = jnp.concatenate([vp_ref[...], vc_ref[...]], axis=0)
    r = lax.broadcasted_iota(jnp.int32, (Q_BLOCK, 2 * Q_BLOCK), 0)
    j = lax.broadcasted_iota(jnp.int32, (Q_BLOCK, 2 * Q_BLOCK), 1)
    dist = Q_BLOCK + r - j
    valid = (dist >= 0) & (dist < WINDOW) & ((j >= Q_BLOCK) | (i > 0))
    group = SWA_HEADS // SWA_KV_HEADS
    for h in range(SWA_HEADS):
        g = h // group
        q_ref = qa_ref if h < group else qb_ref
        c0 = HEAD_DIM * (h % group)
        qh = q_ref[:, c0:c0 + HEAD_DIM] * ATTN_SCALE
        s = _dot_t(qh, kk[:, HEAD_DIM * g:HEAD_DIM * (g + 1)]) + bias_ref[h]
        s = jnp.where(valid, s, NEG)
        sink = sink_ref[h]
        m = jnp.maximum(jnp.max(s, axis=-1, keepdims=True), sink)
        p = jnp.exp(s - m)
        l = jnp.sum(p, axis=-1, keepdims=True) + jnp.exp(sink - m)
        o = _dot(p.astype(BF16), vv[:, HEAD_DIM * g:HEAD_DIM * (g + 1)])
        o_ref[:, HEAD_DIM * h:HEAD_DIM * (h + 1)] = o / l


def _swa_attention(proj, sinks, bias, batch, seq):
    n = proj.shape[0]
    nq = seq // Q_BLOCK
    wide = 2 * LANES
    grid_spec = pltpu.PrefetchScalarGridSpec(
        num_scalar_prefetch=1,
        grid=(batch, nq),
        in_specs=[
            pl.BlockSpec((Q_BLOCK, wide), lambda b, i, s: (b * nq + i, COL_SW_Q // 2)),
            pl.BlockSpec((Q_BLOCK, wide), lambda b, i, s: (b * nq + i, COL_SW_Q // 2 + 1)),
            pl.BlockSpec((Q_BLOCK, LANES), lambda b, i, s: (b * nq + jnp.maximum(i - 1, 0), COL_SW_K)),
            pl.BlockSpec((Q_BLOCK, LANES), lambda b, i, s: (b * nq + i, COL_SW_K)),
            pl.BlockSpec((Q_BLOCK, LANES), lambda b, i, s: (b * nq + jnp.maximum(i - 1, 0), COL_SW_V)),
            pl.BlockSpec((Q_BLOCK, LANES), lambda b, i, s: (b * nq + i, COL_SW_V)),
            pl.BlockSpec((SWA_HEADS, Q_BLOCK, 2 * Q_BLOCK), lambda b, i, s: (0, 0, 0)),
        ],
        out_specs=pl.BlockSpec((Q_BLOCK, SWA_W), lambda b, i, s: (b * nq + i, 0)),
    )
    return pl.pallas_call(
        _swa_kernel,
        grid_spec=grid_spec,
        out_shape=jax.ShapeDtypeStruct((n, SWA_W), F32),
        compiler_params=pltpu.CompilerParams(dimension_semantics=("arbitrary", "arbitrary")),
        name="swa_attention",
    )(sinks, proj, proj, proj, proj, proj, proj, bias)


def _moba_kernel(far_ref, q_ref, k_ref, v_ref, btop_ref, o_ref, kmean_ref, *, group):
    p = pl.program_id(1)
    i = pl.program_id(2)
    seq = k_ref.shape[0]
    tq = q_ref.shape[0]
    nb = seq // MOBA_BLOCK
    top_w = 2 * MOBA_BLOCK
    far_w = group * MOBA_BLOCK
    lane = lax.broadcasted_iota(jnp.int32, (tq, LANES), 1)

    @pl.when(i == 0)
    def _():
        blk = lax.broadcasted_iota(jnp.int32, (LANES, seq), 0)
        pos = lax.broadcasted_iota(jnp.int32, (LANES, seq), 1)
        lo = blk * MOBA_BLOCK
        avg = jnp.where((pos >= lo) & (pos < lo + MOBA_BLOCK), 1.0 / MOBA_BLOCK, 0.0).astype(BF16)
        kmean_ref[...] = _dot(avg, k_ref[...]).astype(BF16)

    cur = (i * Q_BLOCK) // MOBA_BLOCK
    first_row = (i * Q_BLOCK) % MOBA_BLOCK
    q = q_ref[...]
    kmean = kmean_ref[...]

    prev_start = pl.multiple_of(jnp.maximum(cur - 1, 0) * MOBA_BLOCK, MOBA_BLOCK)
    own_start = pl.multiple_of(cur * MOBA_BLOCK, MOBA_BLOCK)
    k_top = jnp.concatenate([k_ref[pl.ds(prev_start, MOBA_BLOCK), :], k_ref[pl.ds(own_start, MOBA_BLOCK), :]], axis=0)
    v_top = jnp.concatenate([v_ref[pl.ds(prev_start, MOBA_BLOCK), :], v_ref[pl.ds(own_start, MOBA_BLOCK), :]], axis=0)
    jt = lax.broadcasted_iota(jnp.int32, (tq, top_w), 1)
    rt = lax.broadcasted_iota(jnp.int32, (tq, top_w), 0)
    prev_half = jt < MOBA_BLOCK
    own_causal = (jt >= MOBA_BLOCK) & ((first_row + rt) >= (jt - MOBA_BLOCK))
    pick_row = lax.broadcasted_iota(jnp.int32, (LANES, LANES), 0)
    far_row = lax.broadcasted_iota(jnp.int32, (LANES, far_w), 0)
    far_blk = lax.broadcasted_iota(jnp.int32, (LANES, far_w), 1) // MOBA_BLOCK

    def update(state, s, vb):
        m, l, acc = state
        m_new = jnp.maximum(m, jnp.max(s, axis=-1, keepdims=True))
        alpha = jnp.exp(m - m_new)
        pexp = jnp.exp(s - m_new)
        l = alpha * l + jnp.sum(pexp, axis=-1, keepdims=True)
        acc = alpha * acc + _dot(pexp.astype(BF16), vb)
        return m_new, l, acc

    qms, sel_far, states = [], [], []
    for hh in range(2):
        hmask = (lane >= HEAD_DIM * hh) & (lane < HEAD_DIM * (hh + 1))
        qg = jnp.where(hmask, q, jnp.zeros_like(q))
        qm = qg * ATTN_SCALE
        gate = jnp.where(lane < cur, _dot_t(qg, kmean), NEG)
        sel = jnp.zeros((tq, LANES), jnp.bool_)
        for _ in range(min(MOBA_TOPK, nb)):
            mx = jnp.max(gate, axis=-1, keepdims=True)
            first = jnp.min(jnp.where(gate == mx, lane, LANES), axis=-1, keepdims=True)
            hit = lane == first
            sel = sel | hit
            gate = jnp.where(hit, -jnp.inf, gate)
        sel_prev = jnp.where(sel & (lane == cur - 1), 1.0, 0.0).astype(BF16)
        rep = _dot(sel_prev, jnp.where(pick_row == cur - 1, 1.0, 0.0).astype(BF16))
        prev_ok = jnp.concatenate([rep] * (top_w // LANES), axis=1) > 0.5
        s = _dot_t(qm, k_top) + btop_ref[hh, 0]
        s = jnp.where(own_causal | (prev_half & prev_ok), s, NEG)
        m = jnp.max(s, axis=-1, keepdims=True)
        pexp = jnp.exp(s - m)
        l = jnp.sum(pexp, axis=-1, keepdims=True)
        acc = _dot(pexp.astype(BF16), v_top)
        qms.append(qm)
        sel_far.append(jnp.where(sel & (lane < cur - 1), 1.0, 0.0).astype(BF16))
        states.append((m, l, acc))

    def far_group(g, states):
        start = pl.multiple_of(g * far_w, far_w)
        kg = k_ref[pl.ds(start, far_w), :]
        vg = v_ref[pl.ds(start, far_w), :]
        pick = jnp.where(far_row == g * group + far_blk, 1.0, 0.0).astype(BF16)
        out = []
        for hh in range(2):
            s = _dot_t(qms[hh], kg) + far_ref[2 * p + hh]
            s = jnp.where(_dot(sel_far[hh], pick) > 0.5, s, NEG)
            out.append(update(states[hh], s, vg))
        return tuple(out)

    n_far = jnp.maximum(cur - 1, 0)
    states = lax.fori_loop(0, (n_far + group - 1) // group, far_group, tuple(states))
    outs = [acc / l for (_, l, acc) in states]
    o_ref[...] = jnp.where(lane < HEAD_DIM, outs[0], outs[1])


def _moba_attention(proj, far_bias, bias_top, batch, seq):
    n = proj.shape[0]
    nq = seq // Q_BLOCK
    pairs = MOBA_HEADS // 2
    per_blk = MOBA_BLOCK // Q_BLOCK
    nb = seq // MOBA_BLOCK
    group = min(MOBA_GROUP, nb)
    assert nb % group == 0
    grid_spec = pltpu.PrefetchScalarGridSpec(
        num_scalar_prefetch=1,
        grid=(batch, pairs, nq),
        in_specs=[
            pl.BlockSpec((Q_BLOCK, LANES), lambda b, p, i, f: (b * nq + i, COL_MB_Q + p)),
            pl.BlockSpec((seq, LANES), lambda b, p, i, f: (b, COL_MB_K + p)),
            pl.BlockSpec((seq, LANES), lambda b, p, i, f: (b, COL_MB_V + p)),
            pl.BlockSpec((2, 1, Q_BLOCK, 2 * MOBA_BLOCK), lambda b, p, i, f: (p, i % per_blk, 0, 0)),
        ],
        out_specs=pl.BlockSpec((Q_BLOCK, LANES), lambda b, p, i, f: (b * nq + i, p)),
        scratch_shapes=[pltpu.VMEM((LANES, LANES), BF16)],
    )
    return pl.pallas_call(
        functools.partial(_moba_kernel, group=group),
        grid_spec=grid_spec,
        out_shape=jax.ShapeDtypeStruct((n, MOBA_W), F32),
        compiler_params=pltpu.CompilerParams(
            dimension_semantics=("arbitrary", "arbitrary", "arbitrary"), vmem_limit_bytes=VMEM_LIMIT),
        name="moba_attention",
    )(far_bias, proj, proj, proj, bias_top)


def _outproj_kernel(osb_ref, osw_ref, omb_ref, x_ref, m_ref, gg_ref, wo_ref, gf_ref, wr_ref, br_ref,
                    xo_ref, h_ref, te_ref, tw_ref, cnt_ref):
    m = m_ref[0]

    @pl.when(pl.program_id(0) == 0)
    def _():
        cnt_ref[...] = jnp.zeros_like(cnt_ref)

    c1, c2 = SB_W, SB_W + SWA_W
    y = _dot(_rms(osb_ref[...], gg_ref[:, :c1]).astype(BF16), wo_ref[:c1, :])
    y = y + _dot(_rms(osw_ref[...], gg_ref[:, c1:c2]).astype(BF16), wo_ref[c1:c2, :])
    y = y + _dot(_rms(omb_ref[...], gg_ref[:, c2:]).astype(BF16), wo_ref[c2:, :])
    x = x_ref[...] + m[2:3] * y
    xo_ref[...] = x
    h = _rms(x, gf_ref[...]) * (1.0 + m[4:5]) + m[3:4]
    h_ref[...] = h
    logits = _dot(h.astype(BF16), wr_ref[...]) + br_ref[...]
    lane = lax.broadcasted_iota(jnp.int32, logits.shape, 1)
    ids = jnp.zeros(logits.shape, jnp.int32)
    wts = jnp.zeros(logits.shape, F32)
    chosen = jnp.zeros(logits.shape, F32)
    top = None
    denom = None
    for r in range(TOP_K):
        mx = jnp.max(logits, axis=-1, keepdims=True)
        first = jnp.min(jnp.where(logits == mx, lane, LANES), axis=-1, keepdims=True)
        hit = lane == first
        logits = jnp.where(hit, -jnp.inf, logits)
        chosen = jnp.where(hit, 1.0, chosen)
        if r == 0:
            top = mx
        e = jnp.exp(mx - top)
        denom = e if r == 0 else denom + e
        ids = jnp.where(lane == r, first, ids)
        wts = jnp.where(lane == r, e, wts)
    te_ref[...] = ids
    tw_ref[...] = wts / denom
    cnt_ref[...] += _dot(jnp.ones((cnt_ref.shape[0], chosen.shape[0]), BF16), chosen.astype(BF16))


def _outproj_router(o_sb, o_sw, o_mb, x, mods, g_group, w_out_b, g_ffn, w_router_b, b_router_p, seq):
    n, d = x.shape
    tm = min(TOKEN_TILE, seq)
    per_batch = seq // tm
    row = lambda i: (i, 0)
    const = lambda i: (0, 0)
    return pl.pallas_call(
        _outproj_kernel,
        grid=(n // tm,),
        in_specs=[
            pl.BlockSpec((tm, SB_W), row),
            pl.BlockSpec((tm, SWA_W), row),
            pl.BlockSpec((tm, MOBA_W), row),
            pl.BlockSpec((tm, d), row),
            pl.BlockSpec((1, 6, d), lambda i: (i // per_batch, 0, 0)),
            pl.BlockSpec((1, d), const),
            pl.BlockSpec((d, d), const),
            pl.BlockSpec((1, d), const),
            pl.BlockSpec((d, LANES), const),
            pl.BlockSpec((1, LANES), const),
        ],
        out_specs=[
            pl.BlockSpec((tm, d), row),
            pl.BlockSpec((tm, d), row),
            pl.BlockSpec((tm, LANES), row),
            pl.BlockSpec((tm, LANES), row),
            pl.BlockSpec((8, LANES), const),
        ],
        out_shape=[
            jax.ShapeDtypeStruct((n, d), F32),
            jax.ShapeDtypeStruct((n, d), F32),
            jax.ShapeDtypeStruct((n, LANES), jnp.int32),
            jax.ShapeDtypeStruct((n, LANES), F32),
            jax.ShapeDtypeStruct((8, LANES), F32),
        ],
        compiler_params=pltpu.CompilerParams(dimension_semantics=("arbitrary",), vmem_limit_bytes=VMEM_LIMIT),
        name="outproj_router",
    )(o_sb, o_sw, o_mb, x, mods, g_group.reshape(1, d), w_out_b, g_ffn.reshape(1, d), w_router_b, b_router_p)


def _moe_kernel(be_ref, nu_ref, xs_ref, wgu_ref, bgu_ref, wd_ref, bd_ref, o_ref, wgu_b, wd_b):
    i = pl.program_id(0)
    d_ff = wd_b.shape[0]

    @pl.when(i < nu_ref[0])
    def _():
        changed = jnp.logical_or(i == 0, be_ref[i] != be_ref[jnp.maximum(i - 1, 0)])

        @pl.when(changed)
        def _():
            wgu_b[...] = wgu_ref[0, 0].astype(BF16)
            wd_b[...] = wd_ref[0, 0].astype(BF16)

        gu = _dot(xs_ref[...].astype(BF16), wgu_b[...]) + bgu_ref[0, 0]
        gate = jnp.minimum(gu[:, :d_ff], SWIGLU_LIMIT)
        up = jnp.clip(gu[:, d_ff:], -SWIGLU_LIMIT, SWIGLU_LIMIT)
        act = (up + 1.0) * gate * (1.0 / (1.0 + jnp.exp(-SWIGLU_ALPHA * gate)))
        o_ref[...] = _dot(act.astype(BF16), wd_b[...]) + bd_ref[0, 0]

    @pl.when(i >= nu_ref[0])
    def _():
        o_ref[...] = jnp.zeros_like(o_ref)


def _moe_experts(xs, block_e, n_used, layer, w_gate_up, b_gate_up, w_down, b_down):
    n_rows, d = xs.shape
    depth, n_exp, _, two_f = w_gate_up.shape
    d_ff = two_f // 2
    n_blocks = n_rows // MOE_ROWS
    blk = lambda i, be, nu: (jnp.minimum(i, nu[0] - 1), 0)
    grid_spec = pltpu.PrefetchScalarGridSpec(
        num_scalar_prefetch=2,
        grid=(n_blocks,),
        in_specs=[
            pl.BlockSpec((MOE_ROWS, d), blk),
            pl.BlockSpec((1, 1, d, two_f), lambda i, be, nu: (layer, be[i], 0, 0)),
            pl.BlockSpec((1, 1, 1, two_f), lambda i, be, nu: (layer, be[i], 0, 0)),
            pl.BlockSpec((1, 1, d_ff, d), lambda i, be, nu: (layer, be[i], 0, 0)),
            pl.BlockSpec((1, 1, 1, d), lambda i, be, nu: (layer, be[i], 0, 0)),
        ],
        out_specs=pl.BlockSpec((MOE_ROWS, d), lambda i, be, nu: (i, 0)),
        scratch_shapes=[
            pltpu.VMEM((d, two_f), BF16),
            pltpu.VMEM((d_ff, d), BF16),
        ],
    )
    return pl.pallas_call(
        _moe_kernel,
        grid_spec=grid_spec,
        out_shape=jax.ShapeDtypeStruct((n_rows, d), F32),
        compiler_params=pltpu.CompilerParams(dimension_semantics=("arbitrary",), vmem_limit_bytes=VMEM_LIMIT),
        name="moe_experts",
    )(block_e, n_used, xs, w_gate_up, b_gate_up.reshape(depth, n_exp, 1, two_f), w_down,
      b_down.reshape(depth, n_exp, 1, d))


def _combine_kernel(pos_ref, ys_hbm, x_ref, tw_ref, m_ref, gfin_ref, o_ref, buf, pos_smem, sem_rows, sem_pos,
                    *, final):
    tm = x_ref.shape[0]
    to_smem = pltpu.make_async_copy(pos_ref, pos_smem, sem_pos)
    to_smem.start()
    to_smem.wait()

    def row_copy(t, k):
        return pltpu.make_async_copy(ys_hbm.at[pl.ds(pos_smem[t, k], 1)], buf.at[k, pl.ds(t, 1)], sem_rows)

    def issue(t, carry):
        for k in range(TOP_K):
            row_copy(t, k).start()
        return carry

    lax.fori_loop(0, tm, issue, 0, unroll=8)

    def drain(t, carry):
        for k in range(TOP_K):
            row_copy(t, k).wait()
        return carry

    lax.fori_loop(0, tm, drain, 0, unroll=8)

    tw = tw_ref[...]
    moe = tw[:, 0:1] * buf[0]
    for k in range(1, TOP_K):
        moe = moe + tw[:, k:k + 1] * buf[k]
    x = x_ref[...] + m_ref[0][5:6] * moe
    if final:
        x = _rms(x, gfin_ref[...])
    o_ref[...] = x


def _moe_combine(ys, pos, x, tw, mods, g_final, seq, final):
    n, d = x.shape
    tm = min(COMBINE_TILE, seq)
    per_batch = seq // tm
    n_tiles = n // tm
    return pl.pallas_call(
        functools.partial(_combine_kernel, final=final),
        grid=(n_tiles,),
        in_specs=[
            pl.BlockSpec((tm, LANES), lambda i: (i, 0)),
            pl.BlockSpec(memory_space=pl.ANY),
            pl.BlockSpec((tm, d), lambda i: (i, 0)),
            pl.BlockSpec((tm, LANES), lambda i: (i, 0)),
            pl.BlockSpec((1, 6, d), lambda i: (i // per_batch, 0, 0)),
            pl.BlockSpec((1, d), lambda i: (0, 0)),
        ],
        out_specs=pl.BlockSpec((tm, d), lambda i: (i, 0)),
        out_shape=jax.ShapeDtypeStruct((n, d), F32),
        scratch_shapes=[pltpu.VMEM((TOP_K, tm, d), F32), pltpu.SMEM((tm, LANES), jnp.int32),
                        pltpu.SemaphoreType.DMA, pltpu.SemaphoreType.DMA],
        compiler_params=pltpu.CompilerParams(dimension_semantics=("arbitrary",), vmem_limit_bytes=VMEM_LIMIT),
        name="moe_combine",
    )(pos, ys, x, tw, mods, g_final.reshape(1, d))


def _row_plan(counts, n_tok):
    cnt = counts[0, :N_EXPERTS].astype(jnp.int32)
    padded = (cnt + MOE_ROWS - 1) // MOE_ROWS * MOE_ROWS
    pad_end = jnp.cumsum(padded)
    pad_start = pad_end - padded
    n_blocks = -(-n_tok * TOP_K // MOE_ROWS) + N_EXPERTS
    n_used = (pad_end[-1] // MOE_ROWS).astype(jnp.int32)
    blk_first = jnp.minimum(jnp.arange(n_blocks, dtype=jnp.int32), n_used - 1) * MOE_ROWS
    block_e = jnp.sum(blk_first[:, None] >= pad_end[None, :], axis=1).astype(jnp.int32)
    pad_end0 = jnp.concatenate([jnp.zeros((1,), jnp.int32), pad_end.astype(jnp.int32)])
    pad_start_row = jnp.zeros((1, LANES), F32).at[0, :N_EXPERTS].set(pad_start.astype(F32))
    return pad_end0, pad_start_row, block_e, n_used.reshape(1), n_blocks * MOE_ROWS


def _dispatch_kernel(pend_ref, te_ref, h_ref, pstart_ref, pos_ref, xs_hbm, carry_ref, zero_ref, pos_smem,
                     sem_rows, sem_misc):
    i = pl.program_id(0)
    tm = te_ref.shape[0]
    lane = lax.broadcasted_iota(jnp.int32, (tm, LANES), 1)

    @pl.when(i == 0)
    def _():
        carry_ref[...] = jnp.zeros_like(carry_ref)
        zero_ref[...] = jnp.zeros_like(zero_ref)

        def tail_copy(e):
            end = pl.multiple_of(pend_ref[e + 1], MOE_ROWS)
            return pltpu.make_async_copy(zero_ref, xs_hbm.at[pl.ds(end - MOE_ROWS, MOE_ROWS)], sem_misc)

        for e in range(N_EXPERTS):
            @pl.when(pend_ref[e + 1] > pend_ref[e])
            def _():
                tail_copy(e).start()
        for e in range(N_EXPERTS):
            @pl.when(pend_ref[e + 1] > pend_ref[e])
            def _():
                tail_copy(e).wait()

        def spare_copy(b):
            return pltpu.make_async_copy(
                zero_ref, xs_hbm.at[pl.ds(pl.multiple_of(b * MOE_ROWS, MOE_ROWS), MOE_ROWS)], sem_misc)

        first_spare = pend_ref[N_EXPERTS] // MOE_ROWS
        n_blocks = xs_hbm.shape[0] // MOE_ROWS

        def start_spare(b, carry):
            spare_copy(b).start()
            return carry

        def wait_spare(b, carry):
            spare_copy(b).wait()
            return carry

        lax.fori_loop(first_spare, n_blocks, start_spare, 0)
        lax.fori_loop(first_spare, n_blocks, wait_spare, 0)

    te = te_ref[...]
    hits = [lane == te[:, k:k + 1] for k in range(TOP_K)]
    cnt = jnp.zeros((tm, LANES), F32)
    for k in range(TOP_K):
        cnt = cnt + jnp.where(hits[k], 1.0, 0.0)
    cnt_b = cnt.astype(BF16)
    r = lax.broadcasted_iota(jnp.int32, (tm, tm), 0)
    c = lax.broadcasted_iota(jnp.int32, (tm, tm), 1)
    before = jnp.where(c < r, 1.0, 0.0).astype(BF16)
    base = _dot(before, cnt_b) + (carry_ref[0:1, :] + pstart_ref[...])
    pos = jnp.zeros((tm, LANES), jnp.int32)
    for k in range(TOP_K):
        pk = jnp.sum(jnp.where(hits[k], base, 0.0), axis=-1, keepdims=True).astype(jnp.int32)
        pos = jnp.where(lane == k, pk, pos)
    pos_ref[...] = pos
    carry_ref[...] = carry_ref[...] + _dot(jnp.ones((carry_ref.shape[0], tm), BF16), cnt_b)

    to_smem = pltpu.make_async_copy(pos_ref, pos_smem, sem_misc)
    to_smem.start()
    to_smem.wait()

    def row_copy(t, k):
        return pltpu.make_async_copy(h_ref.at[pl.ds(t, 1)], xs_hbm.at[pl.ds(pos_smem[t, k], 1)], sem_rows)

    def issue(t, carry):
        for k in range(TOP_K):
            row_copy(t, k).start()
        return carry

    lax.fori_loop(0, tm, issue, 0, unroll=8)

    def drain(t, carry):
        for k in range(TOP_K):
            row_copy(t, k).wait()
        return carry

    lax.fori_loop(0, tm, drain, 0, unroll=8)


def _moe_dispatch(top_e, h, pad_end0, pad_start_row, n_rows, seq):
    n, d = h.shape
    tm = min(DISPATCH_TILE, seq)
    grid_spec = pltpu.PrefetchScalarGridSpec(
        num_scalar_prefetch=1,
        grid=(n // tm,),
        in_specs=[
            pl.BlockSpec((tm, LANES), lambda i, pe: (i, 0)),
            pl.BlockSpec((tm, d), lambda i, pe: (i, 0)),
            pl.BlockSpec((1, LANES), lambda i, pe: (0, 0)),
        ],
        out_specs=[
            pl.BlockSpec((tm, LANES), lambda i, pe: (i, 0)),
            pl.BlockSpec(memory_space=pl.ANY),
        ],
        scratch_shapes=[
            pltpu.VMEM((8, LANES), F32),
            pltpu.VMEM((MOE_ROWS, d), F32),
            pltpu.SMEM((tm, LANES), jnp.int32),
            pltpu.SemaphoreType.DMA,
            pltpu.SemaphoreType.DMA,
        ],
    )
    return pl.pallas_call(
        _dispatch_kernel,
        grid_spec=grid_spec,
        out_shape=[jax.ShapeDtypeStruct((n, LANES), jnp.int32), jax.ShapeDtypeStruct((n_rows, d), F32)],
        compiler_params=pltpu.CompilerParams(dimension_semantics=("arbitrary",), vmem_limit_bytes=VMEM_LIMIT),
        name="moe_dispatch",
    )(pad_end0, top_e, h, pad_start_row)


def _t5_bucket(dist):
    n = jnp.maximum(dist, 0)
    max_exact = NUM_BUCKETS // 2
    nf = jnp.maximum(n, 1).astype(F32)
    large = max_exact + (jnp.log(nf / max_exact) / math.log(MAX_DISTANCE / max_exact)
                         * (NUM_BUCKETS - max_exact)).astype(jnp.int32)
    large = jnp.minimum(large, NUM_BUCKETS - 1)
    return jnp.where(n < max_exact, n, large)


def _bias_tables(rel_bias):
    r = jnp.arange(Q_BLOCK)[:, None]
    j2 = jnp.arange(2 * Q_BLOCK)[None, :]
    swa = rel_bias[:, :SWA_HEADS][_t5_bucket(Q_BLOCK + r - j2)].transpose(2, 0, 1)
    tab = rel_bias[:, SWA_HEADS:]
    jb = jnp.arange(MOBA_BLOCK)[None, None, :]
    t_loc = (jnp.arange(MOBA_BLOCK // Q_BLOCK)[:, None, None] * Q_BLOCK + r[None])
    own = tab[_t5_bucket(t_loc - jb)].transpose(3, 0, 1, 2)
    prev = tab[_t5_bucket(t_loc + MOBA_BLOCK - jb)].transpose(3, 0, 1, 2)
    far = tab[_t5_bucket(jnp.int32(2 * MOBA_BLOCK))]
    top = jnp.concatenate([prev, own], axis=-1)
    return swa.astype(F32), top.astype(F32), far.astype(F32)


def kernel(x, c, w_in, w_out, g_norm_mix, g_norm_ffn, g_group, w_mod, b_mod, swa_sinks, rel_bias,
           w_router, b_router, w_gate_up, b_gate_up, w_down, b_down, g_final):
    batch, seq, d = x.shape
    depth = w_in.shape[0]
    n = batch * seq
    assert seq % MOBA_BLOCK == 0 and seq // MOBA_BLOCK <= LANES and d % LANES == 0

    mods = _modulation(c, w_mod, b_mod)
    bias_swa, bias_top, bias_far = _bias_tables(rel_bias)
    w_in_b = w_in.astype(BF16)
    w_out_b = w_out.astype(BF16)
    w_router_b = jnp.zeros((depth, d, LANES), BF16).at[:, :, :N_EXPERTS].set(w_router.astype(BF16))
    b_router_p = jnp.full((depth, 1, LANES), NEG, F32).at[:, 0, :N_EXPERTS].set(b_router)

    xf = x.reshape(n, d)
    for l in range(depth):
        proj = _inproj(xf, g_norm_mix[l], mods[l], w_in_b[l], seq)
        o_sb = _sb_attention(proj, batch, seq)
        o_sw = _swa_attention(proj, swa_sinks[l], bias_swa, batch, seq)
        o_mb = _moba_attention(proj, bias_far, bias_top, batch, seq)
        xf, h, top_e, top_w, counts = _outproj_router(o_sb, o_sw, o_mb, xf, mods[l], g_group[l], w_out_b[l],
                                                      g_norm_ffn[l], w_router_b[l], b_router_p[l], seq)
        pad_end0, pad_start_row, block_e, n_used, n_rows = _row_plan(counts, n)
        pos, xs = _moe_dispatch(top_e, h, pad_end0, pad_start_row, n_rows, seq)
        ys = _moe_experts(xs, block_e, n_used, l, w_gate_up, b_gate_up, w_down, b_down)
        xf = _moe_combine(ys, pos, xf, top_w, mods[l], g_final, seq, final=(l == depth - 1))
    return xf.reshape(batch, seq, d)
```

```python
import functools
import math

import jax
import jax.numpy as jnp
from jax import lax
from jax.experimental import pallas as pl
from jax.experimental.pallas import tpu as pltpu

F32 = jnp.float32
BF16 = jnp.bfloat16

HEAD_DIM = 64
SB_HEADS = 4
SWA_HEADS = 8
SWA_KV_HEADS = 2
MOBA_HEADS = 4
SB_W = SB_HEADS * HEAD_DIM
SWA_W = SWA_HEADS * HEAD_DIM
SWA_KV_W = SWA_KV_HEADS * HEAD_DIM
MOBA_W = MOBA_HEADS * HEAD_DIM
Q_BLOCK = 128
WINDOW = 128
MOBA_BLOCK = 256
MOBA_TOPK = 3
NUM_BUCKETS = 32
MAX_DISTANCE = 128
N_EXPERTS = 32
TOP_K = 4
SWIGLU_LIMIT = 7.0
SWIGLU_ALPHA = 1.702
MOE_BLOCK = 256
EPS = 1e-6
NEG = -1e30
ATTN_SCALE = HEAD_DIM ** -0.5

LANES = 128
COL_SB_Q, COL_SB_K, COL_SB_V = 0, 2, 4
COL_SW_Q, COL_SW_K, COL_SW_V = 6, 10, 11
COL_MB_Q, COL_MB_K, COL_MB_V = 12, 14, 16
IN_W = 18 * LANES

SB_SPAN = 512
SB_QUERIES = 256
MOBA_GROUP = 4
TOKEN_TILE = 512
MOE_ROWS = 512
DISPATCH_TILE = 256
COMBINE_TILE = 256
VMEM_LIMIT = 56 * 1024 * 1024


def _rms(x, g):
    return x * lax.rsqrt(jnp.mean(x * x, axis=-1, keepdims=True) + EPS) * g


def _dot_t(a, b):
    return lax.dot_general(a, b, (((1,), (1,)), ((), ())), preferred_element_type=F32)


def _dot(a, b):
    return jnp.dot(a, b, preferred_element_type=F32)


def _mod_kernel(c_ref, w_ref, b_ref, o_ref):
    c = c_ref[...]
    ca = c * (1.0 / (1.0 + jnp.exp(-c)))
    o_ref[0] = _dot(ca, w_ref[0]) + b_ref[0]


def _modulation(c, w_mod, b_mod):
    depth, d, six_d = w_mod.shape
    b = c.shape[0]
    rows = 8
    c8 = jnp.zeros((rows, d), F32).at[:b].set(c)
    tn = six_d // 6
    out = pl.pallas_call(
        _mod_kernel,
        grid=(depth, six_d // tn),
        in_specs=[
            pl.BlockSpec((rows, d), lambda l, j: (0, 0)),
            pl.BlockSpec((1, d, tn), lambda l, j: (l, 0, j)),
            pl.BlockSpec((1, 1, tn), lambda l, j: (l, 0, j)),
        ],
        out_specs=pl.BlockSpec((1, rows, tn), lambda l, j: (l, 0, j)),
        out_shape=jax.ShapeDtypeStruct((depth, rows, six_d), F32),
        name="adaln_mod",
    )(c8, w_mod, b_mod.reshape(depth, 1, six_d))
    return out[:, :b].reshape(depth, b, 6, d)


def _inproj_kernel(x_ref, g_ref, m_ref, w_ref, o_ref):
    m = m_ref[0]
    h = _rms(x_ref[...], g_ref[...]) * (1.0 + m[1:2]) + m[0:1]
    hb = h.astype(BF16)
    step = 2 * LANES
    for j in range(IN_W // step):
        o_ref[:, j * step:(j + 1) * step] = _dot(hb, w_ref[:, j * step:(j + 1) * step]).astype(BF16)


def _inproj(x, g, mods, w_in_b, seq):
    n, d = x.shape
    tm = min(TOKEN_TILE, seq)
    per_batch = seq // tm
    return pl.pallas_call(
        _inproj_kernel,
        grid=(n // tm,),
        in_specs=[
            pl.BlockSpec((tm, d), lambda i: (i, 0)),
            pl.BlockSpec((1, d), lambda i: (0, 0)),
            pl.BlockSpec((1, 6, d), lambda i: (i // per_batch, 0, 0)),
            pl.BlockSpec((d, IN_W), lambda i: (0, 0)),
        ],
        out_specs=pl.BlockSpec((tm, IN_W), lambda i: (i, 0)),
        out_shape=jax.ShapeDtypeStruct((n, IN_W), BF16),
        compiler_params=pltpu.CompilerParams(dimension_semantics=("arbitrary",), vmem_limit_bytes=VMEM_LIMIT),
        name="norm_inproj",
    )(x, g.reshape(1, d), mods, w_in_b)


def _sb_kernel(q_ref, k_ref, v_ref, o_ref, *, span):
    i = pl.program_id(2)
    tq = q_ref.shape[0]
    n_chunks = span // LANES
    lane = lax.broadcasted_iota(jnp.int32, (tq, LANES), 1)
    tri_r = lax.broadcasted_iota(jnp.int32, (LANES, LANES), 0)
    tri_c = lax.broadcasted_iota(jnp.int32, (LANES, LANES), 1)
    tri = jnp.where(tri_r > tri_c, 1.0, 0.0).astype(BF16)
    q = q_ref[...] * ATTN_SCALE
    qms = [jnp.where((lane >= HEAD_DIM * hh) & (lane < HEAD_DIM * (hh + 1)), q, jnp.zeros_like(q))
           for hh in range(2)]
    q_start = i * tq
    key_off = lax.broadcasted_iota(jnp.int32, (tq, span), 1) - lax.broadcasted_iota(jnp.int32, (tq, span), 0)

    def do_span(sidx, state, masked):
        start = pl.multiple_of(sidx * span, span)
        ks = k_ref[pl.ds(start, span), :]
        vs = v_ref[pl.ds(start, span), :]
        new_state = []
        for hh in range(2):
            run, acc = state[hh]
            z = _dot_t(qms[hh], ks)
            sp = jnp.maximum(z, 0.0) + jnp.log(1.0 + jnp.exp(-jnp.abs(z)))
            log_1m = -sp
            log_beta = z - sp
            parts = [None] * n_chunks
            for c in reversed(range(n_chunks)):
                cols = slice(c * LANES, (c + 1) * LANES)
                lc = log_1m[:, cols]
                if masked:
                    strict = (key_off[:, cols] + (start - q_start)) < 0
                    lc = jnp.where(strict, lc, 0.0)
                later = _dot(lc.astype(BF16), tri) + run
                a = jnp.exp(log_beta[:, cols] + later)
                if masked:
                    a = jnp.where(strict, a, 0.0)
                parts[c] = a.astype(BF16)
                run = run + jnp.sum(lc, axis=-1, keepdims=True)
            acc = acc + _dot(jnp.concatenate(parts, axis=1), vs)
            new_state.append((run, acc))
        return tuple(new_state)

    init = ((jnp.zeros((tq, 1), F32), jnp.zeros((tq, LANES), F32)),) * 2
    top = q_start // span
    state = do_span(top, init, True)
    state = lax.fori_loop(0, top, lambda jj, st: do_span(top - 1 - jj, st, False), state)
    o_ref[...] = jnp.where(lane < HEAD_DIM, state[0][1], state[1][1])


def _sb_attention(proj, batch, seq):
    n = proj.shape[0]
    span = min(SB_SPAN, seq)
    tq = min(SB_QUERIES, span)
    assert span % tq == 0 and seq % span == 0
    nq = seq // tq
    pairs = SB_HEADS // 2
    return pl.pallas_call(
        functools.partial(_sb_kernel, span=span),
        grid=(batch, pairs, nq),
        in_specs=[
            pl.BlockSpec((tq, LANES), lambda b, p, i: (b * nq + i, COL_SB_Q + p)),
            pl.BlockSpec((seq, LANES), lambda b, p, i: (b, COL_SB_K + p)),
            pl.BlockSpec((seq, LANES), lambda b, p, i: (b, COL_SB_V + p)),
        ],
        out_specs=pl.BlockSpec((tq, LANES), lambda b, p, i: (b * nq + i, p)),
        out_shape=jax.ShapeDtypeStruct((n, SB_W), F32),
        compiler_params=pltpu.CompilerParams(
            dimension_semantics=("arbitrary", "arbitrary", "arbitrary"), vmem_limit_bytes=VMEM_LIMIT),
        name="sb_attention",
    )(proj, proj, proj)


def _swa_kernel(sink_ref, qa_ref, qb_ref, kp_ref, kc_ref, vp_ref, vc_ref, bias_ref, o_ref):
    i = pl.program_id(1)
    kk = jnp.concatenate([kp_ref[...], kc_ref[...]], axis=0)
    vv = jnp.concatenate([vp_ref[...], vc_ref[...]], axis=0)
    r = lax.broadcasted_iota(jnp.int32, (Q_BLOCK, 2 * Q_BLOCK), 0)
    j = lax.broadcasted_iota(jnp.int32, (Q_BLOCK, 2 * Q_BLOCK), 1)
    dist = Q_BLOCK + r - j
    valid = (dist >= 0) & (dist < WINDOW) & ((j >= Q_BLOCK) | (i > 0))
    group = SWA_HEADS // SWA_KV_HEADS
    for h in range(SWA_HEADS):
        g = h // group
        q_ref = qa_ref if h < group else qb_ref
        c0 = HEAD_DIM * (h % group)
        qh = q_ref[:, c0:c0 + HEAD_DIM] * ATTN_SCALE
        s = _dot_t(qh, kk[:, HEAD_DIM * g:HEAD_DIM * (g + 1)]) + bias_ref[h]
        s = jnp.where(valid, s, NEG)
        sink = sink_ref[h]
        m = jnp.maximum(jnp.max(s, axis=-1, keepdims=True), sink)
        p = jnp.exp(s - m)
        l = jnp.sum(p, axis=-1, keepdims=True) + jnp.exp(sink - m)
        o = _dot(p.astype(BF16), vv[:, HEAD_DIM * g:HEAD_DIM * (g + 1)])
        o_ref[:, HEAD_DIM * h:HEAD_DIM * (h + 1)] = o / l


def _swa_attention(proj, sinks, bias, batch, seq):
    n = proj.shape[0]
    nq = seq // Q_BLOCK
    wide = 2 * LANES
    grid_spec = pltpu.PrefetchScalarGridSpec(
        num_scalar_prefetch=1,
        grid=(batch, nq),
        in_specs=[
            pl.BlockSpec((Q_BLOCK, wide), lambda b, i, s: (b * nq + i, COL_SW_Q // 2)),
            pl.BlockSpec((Q_BLOCK, wide), lambda b, i, s: (b * nq + i, COL_SW_Q // 2 + 1)),
            pl.BlockSpec((Q_BLOCK, LANES), lambda b, i, s: (b * nq + jnp.maximum(i - 1, 0), COL_SW_K)),
            pl.BlockSpec((Q_BLOCK, LANES), lambda b, i, s: (b * nq + i, COL_SW_K)),
            pl.BlockSpec((Q_BLOCK, LANES), lambda b, i, s: (b * nq + jnp.maximum(i - 1, 0), COL_SW_V)),
            pl.BlockSpec((Q_BLOCK, LANES), lambda b, i, s: (b * nq + i, COL_SW_V)),
            pl.BlockSpec((SWA_HEADS, Q_BLOCK, 2 * Q_BLOCK), lambda b, i, s: (0, 0, 0)),
        ],
        out_specs=pl.BlockSpec((Q_BLOCK, SWA_W), lambda b, i, s: (b * nq + i, 0)),
    )
    return pl.pallas_call(
        _swa_kernel,
        grid_spec=grid_spec,
        out_shape=jax.ShapeDtypeStruct((n, SWA_W), F32),
        compiler_params=pltpu.CompilerParams(dimension_semantics=("arbitrary", "arbitrary")),
        name="swa_attention",
    )(sinks, proj, proj, proj, proj, proj, proj, bias)


def _moba_kernel(far_ref, q_ref, k_ref, v_ref, btop_ref, o_ref, kmean_ref, rhs_ref, *, group):
    p = pl.program_id(1)
    cur = pl.program_id(2)
    seq = k_ref.shape[0]
    tq = q_ref.shape[0]
    nb = seq // MOBA_BLOCK
    top_w = 2 * MOBA_BLOCK
    far_w = group * MOBA_BLOCK
    lane = lax.broadcasted_iota(jnp.int32, (tq, LANES), 1)

    @pl.when(cur == 0)
    def _():
        blk = lax.broadcasted_iota(jnp.int32, (LANES, seq), 0)
        pos = lax.broadcasted_iota(jnp.int32, (LANES, seq), 1)
        lo = blk * MOBA_BLOCK
        avg = jnp.where((pos >= lo) & (pos < lo + MOBA_BLOCK), 1.0 / MOBA_BLOCK, 0.0).astype(BF16)
        kmean_ref[...] = _dot(avg, k_ref[...]).astype(BF16)
        key_blk = lax.broadcasted_iota(jnp.int32, (seq, LANES), 0) // MOBA_BLOCK
        blk_lane = lax.broadcasted_iota(jnp.int32, (seq, LANES), 1)
        rhs_ref[:, :LANES] = k_ref[...]
        rhs_ref[:, LANES:] = jnp.where(key_blk == blk_lane, 1.0, 0.0).astype(BF16)

    q = q_ref[...]
    kmean = kmean_ref[...]

    prev_start = pl.multiple_of(jnp.maximum(cur - 1, 0) * MOBA_BLOCK, MOBA_BLOCK)
    own_start = pl.multiple_of(cur * MOBA_BLOCK, MOBA_BLOCK)
    rhs_top = jnp.concatenate([rhs_ref[pl.ds(prev_start, MOBA_BLOCK), :], rhs_ref[pl.ds(own_start, MOBA_BLOCK), :]],
                              axis=0)
    v_top = jnp.concatenate([v_ref[pl.ds(prev_start, MOBA_BLOCK), :], v_ref[pl.ds(own_start, MOBA_BLOCK), :]], axis=0)
    jt = lax.broadcasted_iota(jnp.int32, (tq, top_w), 1)
    rt = lax.broadcasted_iota(jnp.int32, (tq, top_w), 0)
    top_ok = jnp.logical_and(jt < MOBA_BLOCK, cur >= 1) | ((jt >= MOBA_BLOCK) & (rt >= jt - MOBA_BLOCK))

    def update(state, s, vb):
        m, l, acc = state
        m_new = jnp.maximum(m, jnp.max(s, axis=-1, keepdims=True))
        alpha = jnp.exp(m - m_new)
        pexp = jnp.exp(s - m_new)
        l = alpha * l + jnp.sum(pexp, axis=-1, keepdims=True)
        acc = alpha * acc + _dot(pexp.astype(BF16), vb)
        return m_new, l, acc

    lhs_far, states = [], []
    for hh in range(2):
        hmask = (lane >= HEAD_DIM * hh) & (lane < HEAD_DIM * (hh + 1))
        qg = jnp.where(hmask, q, jnp.zeros_like(q))
        qm = qg * ATTN_SCALE
        gate = jnp.where(lane < cur, _dot_t(qg, kmean), NEG)
        sel = jnp.zeros((tq, LANES), jnp.bool_)
        for _ in range(min(MOBA_TOPK, nb)):
            mx = jnp.max(gate, axis=-1, keepdims=True)
            first = jnp.min(jnp.where(gate == mx, lane, LANES), axis=-1, keepdims=True)
            hit = lane == first
            sel = sel | hit
            gate = jnp.where(hit, -jnp.inf, gate)
        open_top = (sel & (lane == cur - 1)) | (lane == cur)
        open_far = sel & (lane < cur - 1)
        lhs_top = jnp.concatenate([qm, jnp.where(open_top, 0.0, NEG).astype(BF16)], axis=1)
        s = _dot_t(lhs_top, rhs_top) + btop_ref[hh]
        s = jnp.where(top_ok, s, NEG)
        m = jnp.max(s, axis=-1, keepdims=True)
        pexp = jnp.exp(s - m)
        l = jnp.sum(pexp, axis=-1, keepdims=True)
        acc = _dot(pexp.astype(BF16), v_top)
        lhs_far.append(jnp.concatenate([qm, jnp.where(open_far, 0.0, NEG).astype(BF16)], axis=1))
        states.append((m, l, acc))

    def far_group(g, states):
        start = pl.multiple_of(g * far_w, far_w)
        rhs = rhs_ref[pl.ds(start, far_w), :]
        vg = v_ref[pl.ds(start, far_w), :]
        return tuple(update(states[hh], _dot_t(lhs_far[hh], rhs) + far_ref[2 * p + hh], vg) for hh in range(2))

    n_far = jnp.maximum(cur - 1, 0)
    states = lax.fori_loop(0, (n_far + group - 1) // group, far_group, tuple(states))
    outs = [acc / l for (_, l, acc) in states]
    o_ref[...] = jnp.where(lane < HEAD_DIM, outs[0], outs[1])


def _moba_attention(proj, far_bias, bias_top, batch, seq):
    n = proj.shape[0]
    nq = seq // MOBA_BLOCK
    pairs = MOBA_HEADS // 2
    nb = seq // MOBA_BLOCK
    group = min(MOBA_GROUP, nb)
    assert nb % group == 0
    grid_spec = pltpu.PrefetchScalarGridSpec(
        num_scalar_prefetch=1,
        grid=(batch, pairs, nq),
        in_specs=[
            pl.BlockSpec((MOBA_BLOCK, LANES), lambda b, p, i, f: (b * nq + i, COL_MB_Q + p)),
            pl.BlockSpec((seq, LANES), lambda b, p, i, f: (b, COL_MB_K + p)),
            pl.BlockSpec((seq, LANES), lambda b, p, i, f: (b, COL_MB_V + p)),
            pl.BlockSpec((2, MOBA_BLOCK, 2 * MOBA_BLOCK), lambda b, p, i, f: (p, 0, 0)),
        ],
        out_specs=pl.BlockSpec((MOBA_BLOCK, LANES), lambda b, p, i, f: (b * nq + i, p)),
        scratch_shapes=[pltpu.VMEM((LANES, LANES), BF16), pltpu.VMEM((seq, 2 * LANES), BF16)],
    )
    return pl.pallas_call(
        functools.partial(_moba_kernel, group=group),
        grid_spec=grid_spec,
        out_shape=jax.ShapeDtypeStruct((n, MOBA_W), F32),
        compiler_params=pltpu.CompilerParams(
            dimension_semantics=("arbitrary", "arbitrary", "arbitrary"), vmem_limit_bytes=VMEM_LIMIT),
        name="moba_attention",
    )(far_bias, proj, proj, proj, bias_top)


def _outproj_kernel(osb_ref, osw_ref, omb_ref, x_ref, m_ref, gg_ref, wo_ref, gf_ref, wr_ref, br_ref,
                    xo_ref, h_ref, te_ref, tw_ref, cnt_ref):
    m = m_ref[0]

    @pl.when(pl.program_id(0) == 0)
    def _():
        cnt_ref[...] = jnp.zeros_like(cnt_ref)

    c1, c2 = SB_W, SB_W + SWA_W
    y = _dot(_rms(osb_ref[...], gg_ref[:, :c1]).astype(BF16), wo_ref[:c1, :])
    y = y + _dot(_rms(osw_ref[...], gg_ref[:, c1:c2]).astype(BF16), wo_ref[c1:c2, :])
    y = y + _dot(_rms(omb_ref[...], gg_ref[:, c2:]).astype(BF16), wo_ref[c2:, :])
    x = x_ref[...] + m[2:3] * y
    xo_ref[...] = x
    h = _rms(x, gf_ref[...]) * (1.0 + m[4:5]) + m[3:4]
    h_ref[...] = h
    logits = _dot(h.astype(BF16), wr_ref[...]) + br_ref[...]
    lane = lax.broadcasted_iota(jnp.int32, logits.shape, 1)
    ids = jnp.zeros(logits.shape, jnp.int32)
    wts = jnp.zeros(logits.shape, F32)
    chosen = jnp.zeros(logits.shape, F32)
    top = None
    denom = None
    for r in range(TOP_K):
        mx = jnp.max(logits, axis=-1, keepdims=True)
        first = jnp.min(jnp.where(logits == mx, lane, LANES), axis=-1, keepdims=True)
        hit = lane == first
        logits = jnp.where(hit, -jnp.inf, logits)
        chosen = jnp.where(hit, 1.0, chosen)
        if r == 0:
            top = mx
        e = jnp.exp(mx - top)
        denom = e if r == 0 else denom + e
        ids = jnp.where(lane == r, first, ids)
        wts = jnp.where(lane == r, e, wts)
    te_ref[...] = ids
    tw_ref[...] = wts / denom
    cnt_ref[...] += _dot(jnp.ones((cnt_ref.shape[0], chosen.shape[0]), BF16), chosen.astype(BF16))


def _outproj_router(o_sb, o_sw, o_mb, x, mods, g_group, w_out_b, g_ffn, w_router_b, b_router_p, seq):
    n, d = x.shape
    tm = min(TOKEN_TILE, seq)
    per_batch = seq // tm
    row = lambda i: (i, 0)
    const = lambda i: (0, 0)
    return pl.pallas_call(
        _outproj_kernel,
        grid=(n // tm,),
        in_specs=[
            pl.BlockSpec((tm, SB_W), row),
            pl.BlockSpec((tm, SWA_W), row),
            pl.BlockSpec((tm, MOBA_W), row),
            pl.BlockSpec((tm, d), row),
            pl.BlockSpec((1, 6, d), lambda i: (i // per_batch, 0, 0)),
            pl.BlockSpec((1, d), const),
            pl.BlockSpec((d, d), const),
            pl.BlockSpec((1, d), const),
            pl.BlockSpec((d, LANES), const),
            pl.BlockSpec((1, LANES), const),
        ],
        out_specs=[
            pl.BlockSpec((tm, d), row),
            pl.BlockSpec((tm, d), row),
            pl.BlockSpec((tm, LANES), row),
            pl.BlockSpec((tm, LANES), row),
            pl.BlockSpec((8, LANES), const),
        ],
        out_shape=[
            jax.ShapeDtypeStruct((n, d), F32),
            jax.ShapeDtypeStruct((n, d), F32),
            jax.ShapeDtypeStruct((n, LANES), jnp.int32),
            jax.ShapeDtypeStruct((n, LANES), F32),
            jax.ShapeDtypeStruct((8, LANES), F32),
        ],
        compiler_params=pltpu.CompilerParams(dimension_semantics=("arbitrary",), vmem_limit_bytes=VMEM_LIMIT),
        name="outproj_router",
    )(o_sb, o_sw, o_mb, x, mods, g_group.reshape(1, d), w_out_b, g_ffn.reshape(1, d), w_router_b, b_router_p)


def _moe_kernel(be_ref, nu_ref, xs_ref, wgu_ref, bgu_ref, wd_ref, bd_ref, o_ref, wgu_b, wd_b):
    i = pl.program_id(0)
    d_ff = wd_b.shape[0]

    @pl.when(i < nu_ref[0])
    def _():
        changed = jnp.logical_or(i == 0, be_ref[i] != be_ref[jnp.maximum(i - 1, 0)])

        @pl.when(changed)
        def _():
            wgu_b[...] = wgu_ref[0, 0].astype(BF16)
            wd_b[...] = wd_ref[0, 0].astype(BF16)

        gu = _dot(xs_ref[...].astype(BF16), wgu_b[...]) + bgu_ref[0, 0]
        gate = jnp.minimum(gu[:, :d_ff], SWIGLU_LIMIT)
        up = jnp.clip(gu[:, d_ff:], -SWIGLU_LIMIT, SWIGLU_LIMIT)
        act = (up + 1.0) * gate * (1.0 / (1.0 + jnp.exp(-SWIGLU_ALPHA * gate)))
        o_ref[...] = _dot(act.astype(BF16), wd_b[...]) + bd_ref[0, 0]

    @pl.when(i >= nu_ref[0])
    def _():
        o_ref[...] = jnp.zeros_like(o_ref)


def _moe_experts(xs, block_e, n_used, layer, w_gate_up, b_gate_up, w_down, b_down):
    n_rows, d = xs.shape
    depth, n_exp, _, two_f = w_gate_up.shape
    d_ff = two_f // 2
    n_blocks = n_rows // MOE_ROWS
    blk = lambda i, be, nu: (jnp.minimum(i, nu[0] - 1), 0)
    grid_spec = pltpu.PrefetchScalarGridSpec(
        num_scalar_prefetch=2,
        grid=(n_blocks,),
        in_specs=[
            pl.BlockSpec((MOE_ROWS, d), blk),
            pl.BlockSpec((1, 1, d, two_f), lambda i, be, nu: (layer, be[i], 0, 0)),
            pl.BlockSpec((1, 1, 1, two_f), lambda i, be, nu: (layer, be[i], 0, 0)),
            pl.BlockSpec((1, 1, d_ff, d), lambda i, be, nu: (layer, be[i], 0, 0)),
            pl.BlockSpec((1, 1, 1, d), lambda i, be, nu: (layer, be[i], 0, 0)),
        ],
        out_specs=pl.BlockSpec((MOE_ROWS, d), lambda i, be, nu: (i, 0)),
        scratch_shapes=[
            pltpu.VMEM((d, two_f), BF16),
            pltpu.VMEM((d_ff, d), BF16),
        ],
    )
    return pl.pallas_call(
        _moe_kernel,
        grid_spec=grid_spec,
        out_shape=jax.ShapeDtypeStruct((n_rows, d), F32),
        compiler_params=pltpu.CompilerParams(dimension_semantics=("arbitrary",), vmem_limit_bytes=VMEM_LIMIT),
        name="moe_experts",
    )(block_e, n_used, xs, w_gate_up, b_gate_up.reshape(depth, n_exp, 1, two_f), w_down,
      b_down.reshape(depth, n_exp, 1, d))


def _combine_kernel(pos_ref, ys_hbm, x_ref, tw_ref, m_ref, gfin_ref, o_ref, buf, pos_smem, sem_rows, sem_pos,
                    *, final):
    tm = x_ref.shape[0]
    to_smem = pltpu.make_async_copy(pos_ref, pos_smem, sem_pos)
    to_smem.start()
    to_smem.wait()

    def row_copy(t, k):
        return pltpu.make_async_copy(ys_hbm.at[pl.ds(pos_smem[t, k], 1)], buf.at[k, pl.ds(t, 1)], sem_rows)

    def issue(t, carry):
        for k in range(TOP_K):
            row_copy(t, k).start()
        return carry

    lax.fori_loop(0, tm, issue, 0, unroll=8)

    def drain(t, carry):
        for k in range(TOP_K):
            row_copy(t, k).wait()
        return carry

    lax.fori_loop(0, tm, drain, 0, unroll=8)

    tw = tw_ref[...]
    moe = tw[:, 0:1] * buf[0]
    for k in range(1, TOP_K):
        moe = moe + tw[:, k:k + 1] * buf[k]
    x = x_ref[...] + m_ref[0][5:6] * moe
    if final:
        x = _rms(x, gfin_ref[...])
    o_ref[...] = x


def _moe_combine(ys, pos, x, tw, mods, g_final, seq, final):
    n, d = x.shape
    tm = min(COMBINE_TILE, seq)
    per_batch = seq // tm
    n_tiles = n // tm
    return pl.pallas_call(
        functools.partial(_combine_kernel, final=final),
        grid=(n_tiles,),
        in_specs=[
            pl.BlockSpec((tm, LANES), lambda i: (i, 0)),
            pl.BlockSpec(memory_space=pl.ANY),
            pl.BlockSpec((tm, d), lambda i: (i, 0)),
            pl.BlockSpec((tm, LANES), lambda i: (i, 0)),
            pl.BlockSpec((1, 6, d), lambda i: (i // per_batch, 0, 0)),
            pl.BlockSpec((1, d), lambda i: (0, 0)),
        ],
        out_specs=pl.BlockSpec((tm, d), lambda i: (i, 0)),
        out_shape=jax.ShapeDtypeStruct((n, d), F32),
        scratch_shapes=[pltpu.VMEM((TOP_K, tm, d), F32), pltpu.SMEM((tm, LANES), jnp.int32),
                        pltpu.SemaphoreType.DMA, pltpu.SemaphoreType.DMA],
        compiler_params=pltpu.CompilerParams(dimension_semantics=("arbitrary",), vmem_limit_bytes=VMEM_LIMIT),
        name="moe_combine",
    )(pos, ys, x, tw, mods, g_final.reshape(1, d))


def _row_plan(counts, n_tok):
    cnt = counts[0, :N_EXPERTS].astype(jnp.int32)
    padded = (cnt + MOE_ROWS - 1) // MOE_ROWS * MOE_ROWS
    pad_end = jnp.cumsum(padded)
    pad_start = pad_end - padded
    n_blocks = -(-n_tok * TOP_K // MOE_ROWS) + N_EXPERTS
    n_used = (pad_end[-1] // MOE_ROWS).astype(jnp.int32)
    blk_first = jnp.minimum(jnp.arange(n_blocks, dtype=jnp.int32), n_used - 1) * MOE_ROWS
    block_e = jnp.sum(blk_first[:, None] >= pad_end[None, :], axis=1).astype(jnp.int32)
    pad_end0 = jnp.concatenate([jnp.zeros((1,), jnp.int32), pad_end.astype(jnp.int32)])
    pad_start_row = jnp.zeros((1, LANES), F32).at[0, :N_EXPERTS].set(pad_start.astype(F32))
    return pad_end0, pad_start_row, block_e, n_used.reshape(1), n_blocks * MOE_ROWS


def _dispatch_kernel(pend_ref, te_ref, h_ref, pstart_ref, pos_ref, xs_hbm, carry_ref, zero_ref, pos_smem,
                     sem_rows, sem_misc):
    i = pl.program_id(0)
    tm = te_ref.shape[0]
    lane = lax.broadcasted_iota(jnp.int32, (tm, LANES), 1)

    @pl.when(i == 0)
    def _():
        carry_ref[...] = jnp.zeros_like(carry_ref)
        zero_ref[...] = jnp.zeros_like(zero_ref)

        def tail_copy(e):
            end = pl.multiple_of(pend_ref[e + 1], MOE_ROWS)
            return pltpu.make_async_copy(zero_ref, xs_hbm.at[pl.ds(end - MOE_ROWS, MOE_ROWS)], sem_misc)

        for e in range(N_EXPERTS):
            @pl.when(pend_ref[e + 1] > pend_ref[e])
            def _():
                tail_copy(e).start()
        for e in range(N_EXPERTS):
            @pl.when(pend_ref[e + 1] > pend_ref[e])
            def _():
                tail_copy(e).wait()

        def spare_copy(b):
            return pltpu.make_async_copy(
                zero_ref, xs_hbm.at[pl.ds(pl.multiple_of(b * MOE_ROWS, MOE_ROWS), MOE_ROWS)], sem_misc)

        first_spare = pend_ref[N_EXPERTS] // MOE_ROWS
        n_blocks = xs_hbm.shape[0] // MOE_ROWS

        def start_spare(b, carry):
            spare_copy(b).start()
            return carry

        def wait_spare(b, carry):
            spare_copy(b).wait()
            return carry

        lax.fori_loop(first_spare, n_blocks, start_spare, 0)
        lax.fori_loop(first_spare, n_blocks, wait_spare, 0)

    te = te_ref[...]
    hits = [lane == te[:, k:k + 1] for k in range(TOP_K)]
    cnt = jnp.zeros((tm, LANES), F32)
    for k in range(TOP_K):
        cnt = cnt + jnp.where(hits[k], 1.0, 0.0)
    cnt_b = cnt.astype(BF16)
    r = lax.broadcasted_iota(jnp.int32, (tm, tm), 0)
    c = lax.broadcasted_iota(jnp.int32, (tm, tm), 1)
    before = jnp.where(c < r, 1.0, 0.0).astype(BF16)
    base = _dot(before, cnt_b) + (carry_ref[0:1, :] + pstart_ref[...])
    pos = jnp.zeros((tm, LANES), jnp.int32)
    for k in range(TOP_K):
        pk = jnp.sum(jnp.where(hits[k], base, 0.0), axis=-1, keepdims=True).astype(jnp.int32)
        pos = jnp.where(lane == k, pk, pos)
    pos_ref[...] = pos
    carry_ref[...] = carry_ref[...] + _dot(jnp.ones((carry_ref.shape[0], tm), BF16), cnt_b)

    to_smem = pltpu.make_async_copy(pos_ref, pos_smem, sem_misc)
    to_smem.start()
    to_smem.wait()

    def row_copy(t, k):
        return pltpu.make_async_copy(h_ref.at[pl.ds(t, 1)], xs_hbm.at[pl.ds(pos_smem[t, k], 1)], sem_rows)

    def issue(t, carry):
        for k in range(TOP_K):
            row_copy(t, k).start()
        return carry

    lax.fori_loop(0, tm, issue, 0, unroll=8)

    def drain(t, carry):
        for k in range(TOP_K):
            row_copy(t, k).wait()
        return carry

    lax.fori_loop(0, tm, drain, 0, unroll=8)


def _moe_dispatch(top_e, h, pad_end0, pad_start_row, n_rows, seq):
    n, d = h.shape
    tm = min(DISPATCH_TILE, seq)
    grid_spec = pltpu.PrefetchScalarGridSpec(
        num_scalar_prefetch=1,
        grid=(n // tm,),
        in_specs=[
            pl.BlockSpec((tm, LANES), lambda i, pe: (i, 0)),
            pl.BlockSpec((tm, d), lambda i, pe: (i, 0)),
            pl.BlockSpec((1, LANES), lambda i, pe: (0, 0)),
        ],
        out_specs=[
            pl.BlockSpec((tm, LANES), lambda i, pe: (i, 0)),
            pl.BlockSpec(memory_space=pl.ANY),
        ],
        scratch_shapes=[
            pltpu.VMEM((8, LANES), F32),
            pltpu.VMEM((MOE_ROWS, d), F32),
            pltpu.SMEM((tm, LANES), jnp.int32),
            pltpu.SemaphoreType.DMA,
            pltpu.SemaphoreType.DMA,
        ],
    )
    return pl.pallas_call(
        _dispatch_kernel,
        grid_spec=grid_spec,
        out_shape=[jax.ShapeDtypeStruct((n, LANES), jnp.int32), jax.ShapeDtypeStruct((n_rows, d), F32)],
        compiler_params=pltpu.CompilerParams(dimension_semantics=("arbitrary",), vmem_limit_bytes=VMEM_LIMIT),
        name="moe_dispatch",
    )(pad_end0, top_e, h, pad_start_row)


def _t5_bucket(dist):
    n = jnp.maximum(dist, 0)
    max_exact = NUM_BUCKETS // 2
    nf = jnp.maximum(n, 1).astype(F32)
    large = max_exact + (jnp.log(nf / max_exact) / math.log(MAX_DISTANCE / max_exact)
                         * (NUM_BUCKETS - max_exact)).astype(jnp.int32)
    large = jnp.minimum(large, NUM_BUCKETS - 1)
    return jnp.where(n < max_exact, n, large)


def _bias_lookup(table, dist):
    onehot = (_t5_bucket(dist)[..., None] == jnp.arange(NUM_BUCKETS)).astype(F32)
    out = jnp.einsum("...k,kh->h...", onehot, table.astype(F32), precision=lax.Precision.HIGHEST)
    return out.astype(F32)


def _bias_tables(rel_bias):
    r = jnp.arange(Q_BLOCK)[:, None]
    j2 = jnp.arange(2 * Q_BLOCK)[None, :]
    swa = _bias_lookup(rel_bias[:, :SWA_HEADS], Q_BLOCK + r - j2)
    tab = rel_bias[:, SWA_HEADS:]
    rb = jnp.arange(MOBA_BLOCK)[:, None]
    jb = jnp.arange(2 * MOBA_BLOCK)[None, :]
    top = _bias_lookup(tab, rb + MOBA_BLOCK - jb)
    far = tab[NUM_BUCKETS - 1]
    return swa, top, far.astype(F32)


def kernel(x, c, w_in, w_out, g_norm_mix, g_norm_ffn, g_group, w_mod, b_mod, swa_sinks, rel_bias,
           w_router, b_router, w_gate_up, b_gate_up, w_down, b_down, g_final):
    batch, seq, d = x.shape
    depth = w_in.shape[0]
    n = batch * seq
    assert seq % MOBA_BLOCK == 0 and seq // MOBA_BLOCK <= LANES and d % LANES == 0

    mods = _modulation(c, w_mod, b_mod)
    bias_swa, bias_top, bias_far = _bias_tables(rel_bias)
    w_in_b = w_in.astype(BF16)
    w_out_b = w_out.astype(BF16)
    w_router_b = jnp.zeros((depth, d, LANES), BF16).at[:, :, :N_EXPERTS].set(w_router.astype(BF16))
    b_router_p = jnp.full((depth, 1, LANES), NEG, F32).at[:, 0, :N_EXPERTS].set(b_router)

    xf = x.reshape(n, d)
    for l in range(depth):
        proj = _inproj(xf, g_norm_mix[l], mods[l], w_in_b[l], seq)
        o_sb = _sb_attention(proj, batch, seq)
        o_sw = _swa_attention(proj, swa_sinks[l], bias_swa, batch, seq)
        o_mb = _moba_attention(proj, bias_far, bias_top, batch, seq)
        xf, h, top_e, top_w, counts = _outproj_router(o_sb, o_sw, o_mb, xf, mods[l], g_group[l], w_out_b[l],
                                                      g_norm_ffn[l], w_router_b[l], b_router_p[l], seq)
        pad_end0, pad_start_row, block_e, n_used, n_rows = _row_plan(counts, n)
        pos, xs = _moe_dispatch(top_e, h, pad_end0, pad_start_row, n_rows, seq)
        ys = _moe_experts(xs, block_e, n_used, l, w_gate_up, b_gate_up, w_down, b_down)
        xf = _moe_combine(ys, pos, xf, top_w, mods[l], g_final, seq, final=(l == depth - 1))
    return xf.reshape(batch, seq, d)
```

```python
import functools
import math

import jax
import jax.numpy as jnp
from jax import lax
from jax.experimental import pallas as pl
from jax.experimental.pallas import tpu as pltpu

F32 = jnp.float32
BF16 = jnp.bfloat16

HEAD_DIM = 64
SB_HEADS = 4
SWA_HEADS = 8
SWA_KV_HEADS = 2
MOBA_HEADS = 4
SB_W = SB_HEADS * HEAD_DIM
SWA_W = SWA_HEADS * HEAD_DIM
SWA_KV_W = SWA_KV_HEADS * HEAD_DIM
MOBA_W = MOBA_HEADS * HEAD_DIM
Q_BLOCK = 128
WINDOW = 128
MOBA_BLOCK = 256
MOBA_TOPK = 3
NUM_BUCKETS = 32
MAX_DISTANCE = 128
N_EXPERTS = 32
TOP_K = 4
SWIGLU_LIMIT = 7.0
SWIGLU_ALPHA = 1.702
MOE_BLOCK = 256
EPS = 1e-6
NEG = -1e30
ATTN_SCALE = HEAD_DIM ** -0.5

LANES = 128
SUBLANES = 8
COL_SB_Q, COL_SB_K, COL_SB_V = 0, 2, 4
COL_SW_Q, COL_SW_K, COL_SW_V = 6, 10, 11
COL_MB_Q, COL_MB_K, COL_MB_V = 12, 14, 16
IN_W = 18 * LANES

SB_SPAN = 512
SB_QUERIES = 256
MOBA_GROUP = 4
TOKEN_TILE = 512
MOE_ROWS = 512
DISPATCH_TILE = 256
COMBINE_TILE = 256
ROW_GROUP = 8
VMEM_LIMIT = 56 * 1024 * 1024


def _rms(x, g):
    return x * lax.rsqrt(jnp.mean(x * x, axis=-1, keepdims=True) + EPS) * g


def _dot_t(a, b):
    return lax.dot_general(a, b, (((1,), (1,)), ((), ())), preferred_element_type=F32)


def _dot(a, b):
    return jnp.dot(a, b, preferred_element_type=F32)


def _to_row_tiles(ref, x):
    rows = x.shape[0]
    for s in range(SUBLANES):
        ref[pl.ds(s, rows, stride=SUBLANES), :] = x[:, s * LANES:(s + 1) * LANES]


def _from_row_tiles(ref, rows):
    return jnp.concatenate([ref[pl.ds(s, rows, stride=SUBLANES), :] for s in range(SUBLANES)], axis=1)


def _mod_kernel(c_ref, w_ref, b_ref, o_ref):
    c = c_ref[...]
    ca = c * (1.0 / (1.0 + jnp.exp(-c)))
    o_ref[0] = _dot(ca, w_ref[0]) + b_ref[0]


def _modulation(c, w_mod, b_mod):
    depth, d, six_d = w_mod.shape
    b = c.shape[0]
    rows = 8
    c8 = jnp.zeros((rows, d), F32).at[:b].set(c)
    tn = six_d // 6
    out = pl.pallas_call(
        _mod_kernel,
        grid=(depth, six_d // tn),
        in_specs=[
            pl.BlockSpec((rows, d), lambda l, j: (0, 0)),
            pl.BlockSpec((1, d, tn), lambda l, j: (l, 0, j)),
            pl.BlockSpec((1, 1, tn), lambda l, j: (l, 0, j)),
        ],
        out_specs=pl.BlockSpec((1, rows, tn), lambda l, j: (l, 0, j)),
        out_shape=jax.ShapeDtypeStruct((depth, rows, six_d), F32),
        name="adaln_mod",
    )(c8, w_mod, b_mod.reshape(depth, 1, six_d))
    return out[:, :b].reshape(depth, b, 6, d)


def _inproj_kernel(x_ref, g_ref, m_ref, w_ref, o_ref):
    m = m_ref[0]
    h = _rms(x_ref[...], g_ref[...]) * (1.0 + m[1:2]) + m[0:1]
    hb = h.astype(BF16)
    step = 2 * LANES
    for j in range(IN_W // step):
        o_ref[:, j * step:(j + 1) * step] = _dot(hb, w_ref[:, j * step:(j + 1) * step]).astype(BF16)


def _inproj(x, g, mods, w_in_b, seq):
    n, d = x.shape
    tm = min(TOKEN_TILE, seq)
    per_batch = seq // tm
    return pl.pallas_call(
        _inproj_kernel,
        grid=(n // tm,),
        in_specs=[
            pl.BlockSpec((tm, d), lambda i: (i, 0)),
            pl.BlockSpec((1, d), lambda i: (0, 0)),
            pl.BlockSpec((1, 6, d), lambda i: (i // per_batch, 0, 0)),
            pl.BlockSpec((d, IN_W), lambda i: (0, 0)),
        ],
        out_specs=pl.BlockSpec((tm, IN_W), lambda i: (i, 0)),
        out_shape=jax.ShapeDtypeStruct((n, IN_W), BF16),
        compiler_params=pltpu.CompilerParams(dimension_semantics=("arbitrary",), vmem_limit_bytes=VMEM_LIMIT),
        name="norm_inproj",
    )(x, g.reshape(1, d), mods, w_in_b)


def _sb_kernel(q_ref, k_ref, v_ref, o_ref, *, span):
    i = pl.program_id(2)
    tq = q_ref.shape[0]
    n_chunks = span // LANES
    lane = lax.broadcasted_iota(jnp.int32, (tq, LANES), 1)
    tri_r = lax.broadcasted_iota(jnp.int32, (LANES, LANES), 0)
    tri_c = lax.broadcasted_iota(jnp.int32, (LANES, LANES), 1)
    tri = jnp.where(tri_r > tri_c, 1.0, 0.0).astype(BF16)
    q = q_ref[...] * ATTN_SCALE
    qms = [jnp.where((lane >= HEAD_DIM * hh) & (lane < HEAD_DIM * (hh + 1)), q, jnp.zeros_like(q))
           for hh in range(2)]
    q_start = i * tq
    key_off = lax.broadcasted_iota(jnp.int32, (tq, span), 1) - lax.broadcasted_iota(jnp.int32, (tq, span), 0)

    def do_span(sidx, state, masked):
        start = pl.multiple_of(sidx * span, span)
        ks = k_ref[pl.ds(start, span), :]
        vs = v_ref[pl.ds(start, span), :]
        new_state = []
        for hh in range(2):
            run, acc = state[hh]
            z = _dot_t(qms[hh], ks)
            sp = jnp.maximum(z, 0.0) + jnp.log(1.0 + jnp.exp(-jnp.abs(z)))
            log_1m = -sp
            log_beta = z - sp
            parts = [None] * n_chunks
            for c in reversed(range(n_chunks)):
                cols = slice(c * LANES, (c + 1) * LANES)
                lc = log_1m[:, cols]
                if masked:
                    strict = (key_off[:, cols] + (start - q_start)) < 0
                    lc = jnp.where(strict, lc, 0.0)
                later = _dot(lc.astype(BF16), tri) + run
                a = jnp.exp(log_beta[:, cols] + later)
                if masked:
                    a = jnp.where(strict, a, 0.0)
                parts[c] = a.astype(BF16)
                run = run + jnp.sum(lc, axis=-1, keepdims=True)
            acc = acc + _dot(jnp.concatenate(parts, axis=1), vs)
            new_state.append((run, acc))
        return tuple(new_state)

    init = ((jnp.zeros((tq, 1), F32), jnp.zeros((tq, LANES), F32)),) * 2
    top = q_start // span
    state = do_span(top, init, True)
    state = lax.fori_loop(0, top, lambda jj, st: do_span(top - 1 - jj, st, False), state)
    o_ref[...] = jnp.where(lane < HEAD_DIM, state[0][1], state[1][1])


def _sb_attention(proj, batch, seq):
    n = proj.shape[0]
    span = min(SB_SPAN, seq)
    tq = min(SB_QUERIES, span)
    assert span % tq == 0 and seq % span == 0
    nq = seq // tq
    pairs = SB_HEADS // 2
    return pl.pallas_call(
        functools.partial(_sb_kernel, span=span),
        grid=(batch, pairs, nq),
        in_specs=[
            pl.BlockSpec((tq, LANES), lambda b, p, i: (b * nq + i, COL_SB_Q + p)),
            pl.BlockSpec((seq, LANES), lambda b, p, i: (b, COL_SB_K + p)),
            pl.BlockSpec((seq, LANES), lambda b, p, i: (b, COL_SB_V + p)),
        ],
        out_specs=pl.BlockSpec((tq, LANES), lambda b, p, i: (b * nq + i, p)),
        out_shape=jax.ShapeDtypeStruct((n, SB_W), F32),
        compiler_params=pltpu.CompilerParams(
            dimension_semantics=("arbitrary", "arbitrary", "arbitrary"), vmem_limit_bytes=VMEM_LIMIT),
        name="sb_attention",
    )(proj, proj, proj)


def _swa_kernel(sink_ref, qa_ref, qb_ref, kp_ref, kc_ref, vp_ref, vc_ref, bias_ref, o_ref):
    i = pl.program_id(1)
    kk = jnp.concatenate([kp_ref[...], kc_ref[...]], axis=0)
    vv = jnp.concatenate([vp_ref[...], vc_ref[...]], axis=0)
    r = lax.broadcasted_iota(jnp.int32, (Q_BLOCK, 2 * Q_BLOCK), 0)
    j = lax.broadcasted_iota(jnp.int32, (Q_BLOCK, 2 * Q_BLOCK), 1)
    dist = Q_BLOCK + r - j
    valid = (dist >= 0) & (dist < WINDOW) & ((j >= Q_BLOCK) | (i > 0))
    group = SWA_HEADS // SWA_KV_HEADS
    for h in range(SWA_HEADS):
        g = h // group
        q_ref = qa_ref if h < group else qb_ref
        c0 = HEAD_DIM * (h % group)
        qh = q_ref[:, c0:c0 + HEAD_DIM] * ATTN_SCALE
        s = _dot_t(qh, kk[:, HEAD_DIM * g:HEAD_DIM * (g + 1)]) + bias_ref[h]
        s = jnp.where(valid, s, NEG)
        sink = sink_ref[h]
        m = jnp.maximum(jnp.max(s, axis=-1, keepdims=True), sink)
        p = jnp.exp(s - m)
        l = jnp.sum(p, axis=-1, keepdims=True) + jnp.exp(sink - m)
        o = _dot(p.astype(BF16), vv[:, HEAD_DIM * g:HEAD_DIM * (g + 1)])
        o_ref[:, HEAD_DIM * h:HEAD_DIM * (h + 1)] = o / l


def _swa_attention(proj, sinks, bias, batch, seq):
    n = proj.shape[0]
    nq = seq // Q_BLOCK
    wide = 2 * LANES
    grid_spec = pltpu.PrefetchScalarGridSpec(
        num_scalar_prefetch=1,
        grid=(batch, nq),
        in_specs=[
            pl.BlockSpec((Q_BLOCK, wide), lambda b, i, s: (b * nq + i, COL_SW_Q // 2)),
            pl.BlockSpec((Q_BLOCK, wide), lambda b, i, s: (b * nq + i, COL_SW_Q // 2 + 1)),
            pl.BlockSpec((Q_BLOCK, LANES), lambda b, i, s: (b * nq + jnp.maximum(i - 1, 0), COL_SW_K)),
            pl.BlockSpec((Q_BLOCK, LANES), lambda b, i, s: (b * nq + i, COL_SW_K)),
            pl.BlockSpec((Q_BLOCK, LANES), lambda b, i, s: (b * nq + jnp.maximum(i - 1, 0), COL_SW_V)),
            pl.BlockSpec((Q_BLOCK, LANES), lambda b, i, s: (b * nq + i, COL_SW_V)),
            pl.BlockSpec((SWA_HEADS, Q_BLOCK, 2 * Q_BLOCK), lambda b, i, s: (0, 0, 0)),
        ],
        out_specs=pl.BlockSpec((Q_BLOCK, SWA_W), lambda b, i, s: (b * nq + i, 0)),
    )
    return pl.pallas_call(
        _swa_kernel,
        grid_spec=grid_spec,
        out_shape=jax.ShapeDtypeStruct((n, SWA_W), F32),
        compiler_params=pltpu.CompilerParams(dimension_semantics=("arbitrary", "arbitrary")),
        name="swa_attention",
    )(sinks, proj, proj, proj, proj, proj, proj, bias)


def _moba_kernel(far_ref, q_ref, k_ref, v_ref, btop_ref, o_ref, kmean_ref, rhs_ref, *, group):
    p = pl.program_id(1)
    cur = pl.program_id(2)
    seq = k_ref.shape[0]
    tq = q_ref.shape[0]
    nb = seq // MOBA_BLOCK
    top_w = 2 * MOBA_BLOCK
    far_w = group * MOBA_BLOCK
    lane = lax.broadcasted_iota(jnp.int32, (tq, LANES), 1)

    @pl.when(cur == 0)
    def _():
        blk = lax.broadcasted_iota(jnp.int32, (LANES, seq), 0)
        pos = lax.broadcasted_iota(jnp.int32, (LANES, seq), 1)
        lo = blk * MOBA_BLOCK
        avg = jnp.where((pos >= lo) & (pos < lo + MOBA_BLOCK), 1.0 / MOBA_BLOCK, 0.0).astype(BF16)
        kmean_ref[...] = _dot(avg, k_ref[...]).astype(BF16)
        key_blk = lax.broadcasted_iota(jnp.int32, (seq, LANES), 0) // MOBA_BLOCK
        blk_lane = lax.broadcasted_iota(jnp.int32, (seq, LANES), 1)
        rhs_ref[:, :LANES] = k_ref[...]
        rhs_ref[:, LANES:] = jnp.where(key_blk == blk_lane, 1.0, 0.0).astype(BF16)

    q = q_ref[...]
    kmean = kmean_ref[...]

    prev_start = pl.multiple_of(jnp.maximum(cur - 1, 0) * MOBA_BLOCK, MOBA_BLOCK)
    own_start = pl.multiple_of(cur * MOBA_BLOCK, MOBA_BLOCK)
    rhs_top = jnp.concatenate([rhs_ref[pl.ds(prev_start, MOBA_BLOCK), :], rhs_ref[pl.ds(own_start, MOBA_BLOCK), :]],
                              axis=0)
    v_top = jnp.concatenate([v_ref[pl.ds(prev_start, MOBA_BLOCK), :], v_ref[pl.ds(own_start, MOBA_BLOCK), :]], axis=0)
    jt = lax.broadcasted_iota(jnp.int32, (tq, top_w), 1)
    rt = lax.broadcasted_iota(jnp.int32, (tq, top_w), 0)
    top_ok = jnp.logical_and(jt < MOBA_BLOCK, cur >= 1) | ((jt >= MOBA_BLOCK) & (rt >= jt - MOBA_BLOCK))

    def update(state, s, vb):
        m, l, acc = state
        m_new = jnp.maximum(m, jnp.max(s, axis=-1, keepdims=True))
        alpha = jnp.exp(m - m_new)
        pexp = jnp.exp(s - m_new)
        l = alpha * l + jnp.sum(pexp, axis=-1, keepdims=True)
        acc = alpha * acc + _dot(pexp.astype(BF16), vb)
        return m_new, l, acc

    lhs_far, states = [], []
    for hh in range(2):
        hmask = (lane >= HEAD_DIM * hh) & (lane < HEAD_DIM * (hh + 1))
        qg = jnp.where(hmask, q, jnp.zeros_like(q))
        qm = qg * ATTN_SCALE
        gate = jnp.where(lane < cur, _dot_t(qg, kmean), NEG)
        sel = jnp.zeros((tq, LANES), jnp.bool_)
        for _ in range(min(MOBA_TOPK, nb)):
            mx = jnp.max(gate, axis=-1, keepdims=True)
            first = jnp.min(jnp.where(gate == mx, lane, LANES), axis=-1, keepdims=True)
            hit = lane == first
            sel = sel | hit
            gate = jnp.where(hit, -jnp.inf, gate)
        open_top = (sel & (lane == cur - 1)) | (lane == cur)
        open_far = sel & (lane < cur - 1)
        lhs_top = jnp.concatenate([qm, jnp.where(open_top, 0.0, NEG).astype(BF16)], axis=1)
        s = _dot_t(lhs_top, rhs_top) + btop_ref[hh]
        s = jnp.where(top_ok, s, NEG)
        m = jnp.max(s, axis=-1, keepdims=True)
        pexp = jnp.exp(s - m)
        l = jnp.sum(pexp, axis=-1, keepdims=True)
        acc = _dot(pexp.astype(BF16), v_top)
        lhs_far.append(jnp.concatenate([qm, jnp.where(open_far, 0.0, NEG).astype(BF16)], axis=1))
        states.append((m, l, acc))

    def far_group(g, states):
        start = pl.multiple_of(g * far_w, far_w)
        rhs = rhs_ref[pl.ds(start, far_w), :]
        vg = v_ref[pl.ds(start, far_w), :]
        return tuple(update(states[hh], _dot_t(lhs_far[hh], rhs) + far_ref[2 * p + hh], vg) for hh in range(2))

    n_far = jnp.maximum(cur - 1, 0)
    states = lax.fori_loop(0, (n_far + group - 1) // group, far_group, tuple(states))
    outs = [acc / l for (_, l, acc) in states]
    o_ref[...] = jnp.where(lane < HEAD_DIM, outs[0], outs[1])


def _moba_attention(proj, far_bias, bias_top, batch, seq):
    n = proj.shape[0]
    nq = seq // MOBA_BLOCK
    pairs = MOBA_HEADS // 2
    nb = seq // MOBA_BLOCK
    group = min(MOBA_GROUP, nb)
    assert nb % group == 0
    grid_spec = pltpu.PrefetchScalarGridSpec(
        num_scalar_prefetch=1,
        grid=(batch, pairs, nq),
        in_specs=[
            pl.BlockSpec((MOBA_BLOCK, LANES), lambda b, p, i, f: (b * nq + i, COL_MB_Q + p)),
            pl.BlockSpec((seq, LANES), lambda b, p, i, f: (b, COL_MB_K + p)),
            pl.BlockSpec((seq, LANES), lambda b, p, i, f: (b, COL_MB_V + p)),
            pl.BlockSpec((2, MOBA_BLOCK, 2 * MOBA_BLOCK), lambda b, p, i, f: (p, 0, 0)),
        ],
        out_specs=pl.BlockSpec((MOBA_BLOCK, LANES), lambda b, p, i, f: (b * nq + i, p)),
        scratch_shapes=[pltpu.VMEM((LANES, LANES), BF16), pltpu.VMEM((seq, 2 * LANES), BF16)],
    )
    return pl.pallas_call(
        functools.partial(_moba_kernel, group=group),
        grid_spec=grid_spec,
        out_shape=jax.ShapeDtypeStruct((n, MOBA_W), F32),
        compiler_params=pltpu.CompilerParams(
            dimension_semantics=("arbitrary", "arbitrary", "arbitrary"), vmem_limit_bytes=VMEM_LIMIT),
        name="moba_attention",
    )(far_bias, proj, proj, proj, bias_top)


def _outproj_kernel(osb_ref, osw_ref, omb_ref, x_ref, m_ref, gg_ref, wo_ref, gf_ref, wr_ref, br_ref,
                    xo_ref, h_ref, te_ref, tw_ref, cnt_ref):
    m = m_ref[0]

    @pl.when(pl.program_id(0) == 0)
    def _():
        cnt_ref[...] = jnp.zeros_like(cnt_ref)

    c1, c2 = SB_W, SB_W + SWA_W
    y = _dot(_rms(osb_ref[...], gg_ref[:, :c1]).astype(BF16), wo_ref[:c1, :])
    y = y + _dot(_rms(osw_ref[...], gg_ref[:, c1:c2]).astype(BF16), wo_ref[c1:c2, :])
    y = y + _dot(_rms(omb_ref[...], gg_ref[:, c2:]).astype(BF16), wo_ref[c2:, :])
    x = x_ref[...] + m[2:3] * y
    xo_ref[...] = x
    h = _rms(x, gf_ref[...]) * (1.0 + m[4:5]) + m[3:4]
    _to_row_tiles(h_ref, h)
    logits =_dot(h.astype(BF16), wr_ref[...]) + br_ref[...]
    lane = lax.broadcasted_iota(jnp.int32, logits.shape, 1)
    ids = jnp.zeros(logits.shape, jnp.int32)
    wts = jnp.zeros(logits.shape, F32)
    chosen = jnp.zeros(logits.shape, F32)
    top = None
    denom = None
    for r in range(TOP_K):
        mx = jnp.max(logits, axis=-1, keepdims=True)
        first = jnp.min(jnp.where(logits == mx, lane, LANES), axis=-1, keepdims=True)
        hit = lane == first
        logits = jnp.where(hit, -jnp.inf, logits)
        chosen = jnp.where(hit, 1.0, chosen)
        if r == 0:
            top = mx
        e = jnp.exp(mx - top)
        denom = e if r == 0 else denom + e
        ids = jnp.where(lane == r, first, ids)
        wts = jnp.where(lane == r, e, wts)
    te_ref[...] = ids
    tw_ref[...] = wts / denom
    cnt_ref[...] += _dot(jnp.ones((cnt_ref.shape[0], chosen.shape[0]), BF16), chosen.astype(BF16))


def _outproj_router(o_sb, o_sw, o_mb, x, mods, g_group, w_out_b, g_ffn, w_router_b, b_router_p, seq):
    n, d = x.shape
    tm = min(TOKEN_TILE, seq)
    per_batch = seq // tm
    row = lambda i: (i, 0)
    const = lambda i: (0, 0)
    return pl.pallas_call(
        _outproj_kernel,
        grid=(n // tm,),
        in_specs=[
            pl.BlockSpec((tm, SB_W), row),
            pl.BlockSpec((tm, SWA_W), row),
            pl.BlockSpec((tm, MOBA_W), row),
            pl.BlockSpec((tm, d), row),
            pl.BlockSpec((1, 6, d), lambda i: (i // per_batch, 0, 0)),
            pl.BlockSpec((1, d), const),
            pl.BlockSpec((d, d), const),
            pl.BlockSpec((1, d), const),
            pl.BlockSpec((d, LANES), const),
            pl.BlockSpec((1, LANES), const),
        ],
        out_specs=[
            pl.BlockSpec((tm, d), row),
            pl.BlockSpec((tm * SUBLANES, LANES), row),
            pl.BlockSpec((tm, LANES), row),
            pl.BlockSpec((tm, LANES), row),
            pl.BlockSpec((8, LANES), const),
        ],
        out_shape=[
            jax.ShapeDtypeStruct((n, d), F32),
            jax.ShapeDtypeStruct((n * SUBLANES, LANES), F32),
            jax.ShapeDtypeStruct((n, LANES), jnp.int32),
            jax.ShapeDtypeStruct((n, LANES), F32),
            jax.ShapeDtypeStruct((8, LANES), F32),
        ],
        compiler_params=pltpu.CompilerParams(dimension_semantics=("arbitrary",), vmem_limit_bytes=VMEM_LIMIT),
        name="outproj_router",
    )(o_sb, o_sw, o_mb, x, mods, g_group.reshape(1, d), w_out_b, g_ffn.reshape(1, d), w_router_b, b_router_p)


def _moe_kernel(be_ref, nu_ref, xs_ref, wgu_ref, bgu_ref, wd_ref, bd_ref, o_ref, wgu_b, wd_b):
    i = pl.program_id(0)
    d_ff = wd_b.shape[0]

    @pl.when(i < nu_ref[0])
    def _():
        changed = jnp.logical_or(i == 0, be_ref[i] != be_ref[jnp.maximum(i - 1, 0)])

        @pl.when(changed)
        def _():
            wgu_b[...] = wgu_ref[0, 0].astype(BF16)
            wd_b[...] = wd_ref[0, 0].astype(BF16)

        x = _from_row_tiles(xs_ref, MOE_ROWS).astype(BF16)
        gu = _dot(x, wgu_b[...]) + bgu_ref[0, 0]
        gate = jnp.minimum(gu[:, :d_ff], SWIGLU_LIMIT)
        up = jnp.clip(gu[:, d_ff:], -SWIGLU_LIMIT, SWIGLU_LIMIT)
        act = (up + 1.0) * gate * (1.0 / (1.0 + jnp.exp(-SWIGLU_ALPHA * gate)))
        _to_row_tiles(o_ref, _dot(act.astype(BF16), wd_b[...]) + bd_ref[0, 0])

    @pl.when(i >= nu_ref[0])
    def _():
        o_ref[...] = jnp.zeros_like(o_ref)


def _moe_experts(xs, block_e, n_used, layer, w_gate_up, b_gate_up, w_down, b_down):
    depth, n_exp, d, two_f = w_gate_up.shape
    n_rows = xs.shape[0] // SUBLANES
    d_ff = two_f // 2
    n_blocks = n_rows // MOE_ROWS
    blk = lambda i, be, nu: (jnp.minimum(i, nu[0] - 1), 0)
    grid_spec = pltpu.PrefetchScalarGridSpec(
        num_scalar_prefetch=2,
        grid=(n_blocks,),
        in_specs=[
            pl.BlockSpec((MOE_ROWS * SUBLANES, LANES), blk),
            pl.BlockSpec((1, 1, d, two_f), lambda i, be, nu: (layer, be[i], 0, 0)),
            pl.BlockSpec((1, 1, 1, two_f), lambda i, be, nu: (layer, be[i], 0, 0)),
            pl.BlockSpec((1, 1, d_ff, d), lambda i, be, nu: (layer, be[i], 0, 0)),
            pl.BlockSpec((1, 1, 1, d), lambda i, be, nu: (layer, be[i], 0, 0)),
        ],
        out_specs=pl.BlockSpec((MOE_ROWS * SUBLANES, LANES), lambda i, be, nu: (i, 0)),
        scratch_shapes=[
            pltpu.VMEM((d, two_f), BF16),
            pltpu.VMEM((d_ff, d), BF16),
        ],
    )
    return pl.pallas_call(
        _moe_kernel,
        grid_spec=grid_spec,
        out_shape=jax.ShapeDtypeStruct((n_rows * SUBLANES, LANES), F32),
        compiler_params=pltpu.CompilerParams(dimension_semantics=("arbitrary",), vmem_limit_bytes=VMEM_LIMIT),
        name="moe_experts",
    )(block_e, n_used, xs, w_gate_up, b_gate_up.reshape(depth, n_exp, 1, two_f), w_down,
      b_down.reshape(depth, n_exp, 1, d))


def _combine_kernel(pos_ref, pos_next_ref, ys_hbm, x_ref, tw_ref, m_ref, gfin_ref, o_ref, buf, pos_smem,
                    sem_rows, sem_pos, *, final):
    i = pl.program_id(0)
    n_steps = pl.num_programs(0)
    tm = x_ref.shape[0]
    slot = i % 2

    def row_copy(s, t, k, row):
        src = pl.multiple_of(row * SUBLANES, SUBLANES)
        dst = pl.multiple_of(t * SUBLANES, SUBLANES)
        return pltpu.make_async_copy(ys_hbm.at[pl.ds(src, SUBLANES)], buf.at[s, k, pl.ds(dst, SUBLANES)],
                                     sem_rows.at[s])

    def for_row_groups(s, act):
        def body(g, carry):
            t0 = g * ROW_GROUP
            rows = [[pos_smem[s, t0 + j, k] for k in range(TOP_K)] for j in range(ROW_GROUP)]
            for j in range(ROW_GROUP):
                for k in range(TOP_K):
                    act(row_copy(s, t0 + j, k, rows[j][k]))
            return carry

        lax.fori_loop(0, tm // ROW_GROUP, body, 0)

    def start_gather(src_ref, s):
        to_smem = pltpu.make_async_copy(src_ref, pos_smem.at[s], sem_pos)
        to_smem.start()
        to_smem.wait()
        for_row_groups(s, lambda cp: cp.start())

    @pl.when(i == 0)
    def _():
        start_gather(pos_ref, 0)

    @pl.when(i + 1 < n_steps)
    def _():
        start_gather(pos_next_ref, 1 - slot)

    for_row_groups(slot, lambda cp: cp.wait())

    tw = tw_ref[...]
    cols = []
    for s in range(SUBLANES):
        acc = tw[:, 0:1] * buf.at[slot, 0][pl.ds(s, tm, stride=SUBLANES), :]
        for k in range(1, TOP_K):
            acc = acc + tw[:, k:k + 1] * buf.at[slot, k][pl.ds(s, tm, stride=SUBLANES), :]
        cols.append(acc)
    x = x_ref[...] + m_ref[0][5:6] * jnp.concatenate(cols, axis=1)
    if final:
        x = _rms(x, gfin_ref[...])
    o_ref[...] = x


def _moe_combine(ys, pos, x, tw, mods, g_final, seq, final):
    n, d = x.shape
    tm = min(COMBINE_TILE, seq)
    per_batch = seq // tm
    n_tiles = n // tm
    return pl.pallas_call(
        functools.partial(_combine_kernel, final=final),
        grid=(n_tiles,),
        in_specs=[
            pl.BlockSpec((tm, LANES), lambda i: (i, 0)),
            pl.BlockSpec((tm, LANES), lambda i: (jnp.minimum(i + 1, n_tiles - 1), 0)),
            pl.BlockSpec(memory_space=pl.ANY),
            pl.BlockSpec((tm, d), lambda i: (i, 0)),
            pl.BlockSpec((tm, LANES), lambda i: (i, 0)),
            pl.BlockSpec((1, 6, d), lambda i: (i // per_batch, 0, 0)),
            pl.BlockSpec((1, d), lambda i: (0, 0)),
        ],
        out_specs=pl.BlockSpec((tm, d), lambda i: (i, 0)),
        out_shape=jax.ShapeDtypeStruct((n, d), F32),
        scratch_shapes=[pltpu.VMEM((2, TOP_K, tm * SUBLANES, LANES), F32), pltpu.SMEM((2, tm, LANES), jnp.int32),
                        pltpu.SemaphoreType.DMA((2,)), pltpu.SemaphoreType.DMA],
        compiler_params=pltpu.CompilerParams(dimension_semantics=("arbitrary",), vmem_limit_bytes=VMEM_LIMIT),
        name="moe_combine",
    )(pos, pos, ys, x, tw, mods, g_final.reshape(1, d))


def _row_plan(counts, n_tok):
    cnt = counts[0, :N_EXPERTS].astype(jnp.int32)
    padded = (cnt + MOE_ROWS - 1) // MOE_ROWS * MOE_ROWS
    pad_end = jnp.cumsum(padded)
    pad_start = pad_end - padded
    n_blocks = -(-n_tok * TOP_K // MOE_ROWS) + N_EXPERTS
    n_used = (pad_end[-1] // MOE_ROWS).astype(jnp.int32)
    blk_first = jnp.minimum(jnp.arange(n_blocks, dtype=jnp.int32), n_used - 1) * MOE_ROWS
    block_e = jnp.sum(blk_first[:, None] >= pad_end[None, :], axis=1).astype(jnp.int32)
    pad_end0 = jnp.concatenate([jnp.zeros((1,), jnp.int32), pad_end.astype(jnp.int32)])
    pad_start_row = jnp.zeros((1, LANES), F32).at[0, :N_EXPERTS].set(pad_start.astype(F32))
    return pad_end0, pad_start_row, block_e, n_used.reshape(1), n_blocks * MOE_ROWS


def _dispatch_kernel(pend_ref, te_ref, h_hbm, pstart_ref, pos_ref, xs_hbm, carry_ref, zero_ref, pos_smem,
                     sem_rows, sem_misc):
    i = pl.program_id(0)
    n_steps = pl.num_programs(0)
    tm = te_ref.shape[0]
    blk = MOE_ROWS * SUBLANES
    lane = lax.broadcasted_iota(jnp.int32, (tm, LANES), 1)

    @pl.when(i == 0)
    def _():
        carry_ref[...] = jnp.zeros_like(carry_ref)
        zero_ref[...] = jnp.zeros_like(zero_ref)

        def tail_copy(e):
            end = pl.multiple_of(pend_ref[e + 1] * SUBLANES, blk)
            return pltpu.make_async_copy(zero_ref, xs_hbm.at[pl.ds(end - blk, blk)], sem_misc)

        for e in range(N_EXPERTS):
            @pl.when(pend_ref[e + 1] > pend_ref[e])
            def _():
                tail_copy(e).start()
        for e in range(N_EXPERTS):
            @pl.when(pend_ref[e + 1] > pend_ref[e])
            def _():
                tail_copy(e).wait()

        def spare_copy(b):
            return pltpu.make_async_copy(zero_ref, xs_hbm.at[pl.ds(pl.multiple_of(b * blk, blk), blk)], sem_misc)

        first_spare = pend_ref[N_EXPERTS] // MOE_ROWS
        n_blocks = xs_hbm.shape[0] // blk

        def start_spare(b, carry):
            spare_copy(b).start()
            return carry

        def wait_spare(b, carry):
            spare_copy(b).wait()
            return carry

        lax.fori_loop(first_spare, n_blocks, start_spare, 0)
        lax.fori_loop(first_spare, n_blocks, wait_spare, 0)

    te = te_ref[...]
    hits = [lane == te[:, k:k + 1] for k in range(TOP_K)]
    cnt = jnp.zeros((tm, LANES), F32)
    for k in range(TOP_K):
        cnt = cnt + jnp.where(hits[k], 1.0, 0.0)
    cnt_b = cnt.astype(BF16)
    r = lax.broadcasted_iota(jnp.int32, (tm, tm), 0)
    c = lax.broadcasted_iota(jnp.int32, (tm, tm), 1)
    before = jnp.where(c < r, 1.0, 0.0).astype(BF16)
    base = _dot(before, cnt_b) + (carry_ref[0:1, :] + pstart_ref[...])
    pos = jnp.zeros((tm, LANES), jnp.int32)
    for k in range(TOP_K):
        pk = jnp.sum(jnp.where(hits[k], base, 0.0), axis=-1, keepdims=True).astype(jnp.int32)
        pos = jnp.where(lane == k, pk, pos)
    pos_ref[...] = pos
    carry_ref[...] = carry_ref[...] + _dot(jnp.ones((carry_ref.shape[0], tm), BF16), cnt_b)

    slot = i % 2
    to_smem = pltpu.make_async_copy(pos_ref, pos_smem.at[slot], sem_misc)
    to_smem.start()
    to_smem.wait()

    def row_copy(step, s, t, row):
        src = pl.multiple_of((step * tm + t) * SUBLANES, SUBLANES)
        dst = pl.multiple_of(row * SUBLANES, SUBLANES)
        return pltpu.make_async_copy(h_hbm.at[pl.ds(src, SUBLANES)], xs_hbm.at[pl.ds(dst, SUBLANES)],
                                     sem_rows.at[s])

    def for_row_groups(step, s, act):
        def body(g, carry):
            t0 = g * ROW_GROUP
            rows = [[pos_smem[s, t0 + j, k] for k in range(TOP_K)] for j in range(ROW_GROUP)]
            for j in range(ROW_GROUP):
                for k in range(TOP_K):
                    act(row_copy(step, s, t0 + j, rows[j][k]))
            return carry

        lax.fori_loop(0, tm // ROW_GROUP, body, 0)

    for_row_groups(i, slot, lambda cp: cp.start())

    def drain(step, s):
        for_row_groups(step, s, lambda cp: cp.wait())

    @pl.when(i >= 1)
    def _():
        drain(i - 1, 1 - slot)

    @pl.when(i == n_steps - 1)
    def _():
        drain(i, slot)


def _moe_dispatch(top_e, h, pad_end0, pad_start_row, n_rows, seq):
    n = top_e.shape[0]
    tm = min(DISPATCH_TILE, seq)
    grid_spec = pltpu.PrefetchScalarGridSpec(
        num_scalar_prefetch=1,
        grid=(n // tm,),
        in_specs=[
            pl.BlockSpec((tm, LANES), lambda i, pe: (i, 0)),
            pl.BlockSpec(memory_space=pl.ANY),
            pl.BlockSpec((1, LANES), lambda i, pe: (0, 0)),
        ],
        out_specs=[
            pl.BlockSpec((tm, LANES), lambda i, pe: (i, 0)),
            pl.BlockSpec(memory_space=pl.ANY),
        ],
        scratch_shapes=[
            pltpu.VMEM((8, LANES), F32),
            pltpu.VMEM((MOE_ROWS * SUBLANES, LANES), F32),
            pltpu.SMEM((2, tm, LANES), jnp.int32),
            pltpu.SemaphoreType.DMA((2,)),
            pltpu.SemaphoreType.DMA,
        ],
    )
    return pl.pallas_call(
        _dispatch_kernel,
        grid_spec=grid_spec,
        out_shape=[jax.ShapeDtypeStruct((n, LANES), jnp.int32),
                   jax.ShapeDtypeStruct((n_rows * SUBLANES, LANES), F32)],
        compiler_params=pltpu.CompilerParams(dimension_semantics=("arbitrary",), vmem_limit_bytes=VMEM_LIMIT),
        name="moe_dispatch",
    )(pad_end0, top_e, h, pad_start_row)


def _t5_bucket(dist):
    n = jnp.maximum(dist, 0)
    max_exact = NUM_BUCKETS // 2
    nf = jnp.maximum(n, 1).astype(F32)
    large = max_exact + (jnp.log(nf / max_exact) / math.log(MAX_DISTANCE / max_exact)
                         * (NUM_BUCKETS - max_exact)).astype(jnp.int32)
    large = jnp.minimum(large, NUM_BUCKETS - 1)
    return jnp.where(n < max_exact, n, large)


def _bias_lookup(table, dist):
    onehot = (_t5_bucket(dist)[..., None] == jnp.arange(NUM_BUCKETS)).astype(F32)
    out = jnp.einsum("...k,kh->h...", onehot, table.astype(F32), precision=lax.Precision.HIGHEST)
    return out.astype(F32)


def _bias_tables(rel_bias):
    r = jnp.arange(Q_BLOCK)[:, None]
    j2 = jnp.arange(2 * Q_BLOCK)[None, :]
    swa = _bias_lookup(rel_bias[:, :SWA_HEADS], Q_BLOCK + r - j2)
    tab = rel_bias[:, SWA_HEADS:]
    rb = jnp.arange(MOBA_BLOCK)[:, None]
    jb = jnp.arange(2 * MOBA_BLOCK)[None, :]
    top = _bias_lookup(tab, rb + MOBA_BLOCK - jb)
    far = tab[NUM_BUCKETS - 1]
    return swa, top, far.astype(F32)


def kernel(x, c, w_in, w_out, g_norm_mix, g_norm_ffn, g_group, w_mod, b_mod, swa_sinks, rel_bias,
           w_router, b_router, w_gate_up, b_gate_up, w_down, b_down, g_final):
    batch, seq, d = x.shape
    depth = w_in.shape[0]
    n = batch * seq
    assert seq % MOBA_BLOCK == 0 and seq // MOBA_BLOCK <= LANES and d == SUBLANES * LANES

    mods = _modulation(c, w_mod, b_mod)
    bias_swa, bias_top, bias_far = _bias_tables(rel_bias)
    w_in_b = w_in.astype(BF16)
    w_out_b = w_out.astype(BF16)
    w_router_b = jnp.zeros((depth, d, LANES), BF16).at[:, :, :N_EXPERTS].set(w_router.astype(BF16))
    b_router_p = jnp.full((depth, 1, LANES), NEG, F32).at[:, 0, :N_EXPERTS].set(b_router)

    xf = x.reshape(n, d)
    for l in range(depth):
        proj = _inproj(xf, g_norm_mix[l], mods[l], w_in_b[l], seq)
        o_sb = _sb_attention(proj, batch, seq)
        o_sw = _swa_attention(proj, swa_sinks[l], bias_swa, batch, seq)
        o_mb = _moba_attention(proj, bias_far, bias_top, batch, seq)
        xf, h, top_e, top_w, counts = _outproj_router(o_sb, o_sw, o_mb, xf, mods[l], g_group[l], w_out_b[l],
                                                      g_norm_ffn[l], w_router_b[l], b_router_p[l], seq)
        pad_end0, pad_start_row, block_e, n_used, n_rows = _row_plan(counts, n)
        pos, xs = _moe_dispatch(top_e, h, pad_end0, pad_start_row, n_rows, seq)
        ys = _moe_experts(xs, block_e, n_used, l, w_gate_up, b_gate_up, w_down, b_down)
        xf = _moe_combine(ys, pos, xf, top_w, mods[l], g_final, seq, final=(l == depth - 1))
    return xf.reshape(batch, seq, d)
```

```python
import functools
import math

import jax
import jax.numpy as jnp
from jax import lax
from jax.experimental import pallas as pl
from jax.experimental.pallas import tpu as pltpu

F32 = jnp.float32
BF16 = jnp.bfloat16

HEAD_DIM = 64
SB_HEADS = 4
SWA_HEADS = 8
SWA_KV_HEADS = 2
MOBA_HEADS = 4
SB_W = SB_HEADS * HEAD_DIM
SWA_W = SWA_HEADS * HEAD_DIM
SWA_KV_W = SWA_KV_HEADS * HEAD_DIM
MOBA_W = MOBA_HEADS * HEAD_DIM
Q_BLOCK = 128
WINDOW = 128
MOBA_BLOCK = 256
MOBA_TOPK = 3
NUM_BUCKETS = 32
MAX_DISTANCE = 128
N_EXPERTS = 32
TOP_K = 4
SWIGLU_LIMIT = 7.0
SWIGLU_ALPHA = 1.702
MOE_BLOCK = 256
EPS = 1e-6
NEG = -1e30
ATTN_SCALE = HEAD_DIM ** -0.5

LANES = 128
SUBLANES = 8
COL_SB_Q, COL_SB_K, COL_SB_V = 0, 2, 4
COL_SW_Q, COL_SW_K, COL_SW_V = 6, 10, 11
COL_MB_Q, COL_MB_K, COL_MB_V = 12, 14, 16
IN_W = 18 * LANES

SB_SPAN = 512
SB_QUERIES = 256
MOBA_GROUP = 4
TOKEN_TILE = 512
MOE_ROWS = 512
DISPATCH_TILE = 256
COMBINE_TILE = 256
ROW_GROUP = 8
VMEM_LIMIT = 56 * 1024 * 1024


def _rms(x, g):
    return x * lax.rsqrt(jnp.mean(x * x, axis=-1, keepdims=True) + EPS) * g


def _dot_t(a, b):
    return lax.dot_general(a, b, (((1,), (1,)), ((), ())), preferred_element_type=F32)


def _dot(a, b):
    return jnp.dot(a, b, preferred_element_type=F32)


def _to_row_tiles(ref, x):
    rows = x.shape[0]
    for s in range(SUBLANES):
        ref[pl.ds(s, rows, stride=SUBLANES), :] = x[:, s * LANES:(s + 1) * LANES]


def _from_row_tiles(ref, rows):
    return jnp.concatenate([ref[pl.ds(s, rows, stride=SUBLANES), :] for s in range(SUBLANES)], axis=1)


def _mod_kernel(c_ref, w_ref, b_ref, o_ref):
    c = c_ref[...]
    ca = c * (1.0 / (1.0 + jnp.exp(-c)))
    o_ref[0] = _dot(ca, w_ref[0]) + b_ref[0]


def _modulation(c, w_mod, b_mod):
    depth, d, six_d = w_mod.shape
    b = c.shape[0]
    rows = 8
    c8 = jnp.zeros((rows, d), F32).at[:b].set(c)
    tn = six_d // 6
    out = pl.pallas_call(
        _mod_kernel,
        grid=(depth, six_d // tn),
        in_specs=[
            pl.BlockSpec((rows, d), lambda l, j: (0, 0)),
            pl.BlockSpec((1, d, tn), lambda l, j: (l, 0, j)),
            pl.BlockSpec((1, 1, tn), lambda l, j: (l, 0, j)),
        ],
        out_specs=pl.BlockSpec((1, rows, tn), lambda l, j: (l, 0, j)),
        out_shape=jax.ShapeDtypeStruct((depth, rows, six_d), F32),
        name="adaln_mod",
    )(c8, w_mod, b_mod.reshape(depth, 1, six_d))
    return out[:, :b].reshape(depth, b, 6, d)


def _inproj_kernel(x_ref, g_ref, m_ref, w_ref, o_ref):
    m = m_ref[0]
    h = _rms(x_ref[...], g_ref[...]) * (1.0 + m[1:2]) + m[0:1]
    hb = h.astype(BF16)
    step = 2 * LANES
    for j in range(IN_W // step):
        o_ref[:, j * step:(j + 1) * step] = _dot(hb, w_ref[:, j * step:(j + 1) * step]).astype(BF16)


def _inproj(x, g, mods, w_in_b, seq):
    n, d = x.shape
    tm = min(TOKEN_TILE, seq)
    per_batch = seq // tm
    return pl.pallas_call(
        _inproj_kernel,
        grid=(n // tm,),
        in_specs=[
            pl.BlockSpec((tm, d), lambda i: (i, 0)),
            pl.BlockSpec((1, d), lambda i: (0, 0)),
            pl.BlockSpec((1, 6, d), lambda i: (i // per_batch, 0, 0)),
            pl.BlockSpec((d, IN_W), lambda i: (0, 0)),
        ],
        out_specs=pl.BlockSpec((tm, IN_W), lambda i: (i, 0)),
        out_shape=jax.ShapeDtypeStruct((n, IN_W), BF16),
        compiler_params=pltpu.CompilerParams(dimension_semantics=("arbitrary",), vmem_limit_bytes=VMEM_LIMIT),
        name="norm_inproj",
    )(x, g.reshape(1, d), mods, w_in_b)


def _sb_kernel(q_ref, k_ref, v_ref, o_ref, *, span):
    i = pl.program_id(2)
    tq = q_ref.shape[0]
    n_chunks = span // LANES
    lane = lax.broadcasted_iota(jnp.int32, (tq, LANES), 1)
    tri_r = lax.broadcasted_iota(jnp.int32, (LANES, LANES), 0)
    tri_c = lax.broadcasted_iota(jnp.int32, (LANES, LANES), 1)
    tri = jnp.where(tri_r > tri_c, 1.0, 0.0).astype(BF16)
    q = q_ref[...] * ATTN_SCALE
    qms = [jnp.where((lane >= HEAD_DIM * hh) & (lane < HEAD_DIM * (hh + 1)), q, jnp.zeros_like(q))
           for hh in range(2)]
    q_start = i * tq
    key_off = lax.broadcasted_iota(jnp.int32, (tq, span), 1) - lax.broadcasted_iota(jnp.int32, (tq, span), 0)

    def do_span(sidx, state, masked):
        start = pl.multiple_of(sidx * span, span)
        ks = k_ref[pl.ds(start, span), :]
        vs = v_ref[pl.ds(start, span), :]
        new_state = []
        for hh in range(2):
            run, acc = state[hh]
            z = _dot_t(qms[hh], ks)
            sp = jnp.maximum(z, 0.0) + jnp.log(1.0 + jnp.exp(-jnp.abs(z)))
            log_1m = -sp
            log_beta = z - sp
            parts = [None] * n_chunks
            for c in reversed(range(n_chunks)):
                cols = slice(c * LANES, (c + 1) * LANES)
                lc = log_1m[:, cols]
                if masked:
                    strict = (key_off[:, cols] + (start - q_start)) < 0
                    lc = jnp.where(strict, lc, 0.0)
                later = _dot(lc.astype(BF16), tri) + run
                a = jnp.exp(log_beta[:, cols] + later)
                if masked:
                    a = jnp.where(strict, a, 0.0)
                parts[c] = a.astype(BF16)
                run = run + jnp.sum(lc, axis=-1, keepdims=True)
            acc = acc + _dot(jnp.concatenate(parts, axis=1), vs)
            new_state.append((run, acc))
        return tuple(new_state)

    init = ((jnp.zeros((tq, 1), F32), jnp.zeros((tq, LANES), F32)),) * 2
    top = q_start // span
    state = do_span(top, init, True)
    state = lax.fori_loop(0, top, lambda jj, st: do_span(top - 1 - jj, st, False), state)
    o_ref[...] = jnp.where(lane < HEAD_DIM, state[0][1], state[1][1])


def _sb_attention(proj, batch, seq):
    n = proj.shape[0]
    span = min(SB_SPAN, seq)
    tq = min(SB_QUERIES, span)
    assert span % tq == 0 and seq % span == 0
    nq = seq // tq
    pairs = SB_HEADS // 2
    return pl.pallas_call(
        functools.partial(_sb_kernel, span=span),
        grid=(batch, pairs, nq),
        in_specs=[
            pl.BlockSpec((tq, LANES), lambda b, p, i: (b * nq + i, COL_SB_Q + p)),
            pl.BlockSpec((seq, LANES), lambda b, p, i: (b, COL_SB_K + p)),
            pl.BlockSpec((seq, LANES), lambda b, p, i: (b, COL_SB_V + p)),
        ],
        out_specs=pl.BlockSpec((tq, LANES), lambda b, p, i: (b * nq + i, p)),
        out_shape=jax.ShapeDtypeStruct((n, SB_W), F32),
        compiler_params=pltpu.CompilerParams(
            dimension_semantics=("arbitrary", "arbitrary", "arbitrary"), vmem_limit_bytes=VMEM_LIMIT),
        name="sb_attention",
    )(proj, proj, proj)


def _swa_kernel(sink_ref, qa_ref, qb_ref, kp_ref, kc_ref, vp_ref, vc_ref, bias_ref, o_ref):
    i = pl.program_id(1)
    kk = jnp.concatenate([kp_ref[...], kc_ref[...]], axis=0)
    vv = jnp.concatenate([vp_ref[...], vc_ref[...]], axis=0)
    r = lax.broadcasted_iota(jnp.int32, (Q_BLOCK, 2 * Q_BLOCK), 0)
    j = lax.broadcasted_iota(jnp.int32, (Q_BLOCK, 2 * Q_BLOCK), 1)
    dist = Q_BLOCK + r - j
    valid = (dist >= 0) & (dist < WINDOW) & ((j >= Q_BLOCK) | (i > 0))
    group = SWA_HEADS // SWA_KV_HEADS
    for h in range(SWA_HEADS):
        g = h // group
        q_ref = qa_ref if h < group else qb_ref
        c0 = HEAD_DIM * (h % group)
        qh = q_ref[:, c0:c0 + HEAD_DIM] * ATTN_SCALE
        s = _dot_t(qh, kk[:, HEAD_DIM * g:HEAD_DIM * (g + 1)]) + bias_ref[h]
        s = jnp.where(valid, s, NEG)
        sink = sink_ref[h]
        m = jnp.maximum(jnp.max(s, axis=-1, keepdims=True), sink)
        p = jnp.exp(s - m)
        l = jnp.sum(p, axis=-1, keepdims=True) + jnp.exp(sink - m)
        o = _dot(p.astype(BF16), vv[:, HEAD_DIM * g:HEAD_DIM * (g + 1)])
        o_ref[:, HEAD_DIM * h:HEAD_DIM * (h + 1)] = o / l


def _swa_attention(proj, sinks, bias, batch, seq):
    n = proj.shape[0]
    nq = seq // Q_BLOCK
    wide = 2 * LANES
    grid_spec = pltpu.PrefetchScalarGridSpec(
        num_scalar_prefetch=1,
        grid=(batch, nq),
        in_specs=[
            pl.BlockSpec((Q_BLOCK, wide), lambda b, i, s: (b * nq + i, COL_SW_Q // 2)),
            pl.BlockSpec((Q_BLOCK, wide), lambda b, i, s: (b * nq + i, COL_SW_Q // 2 + 1)),
            pl.BlockSpec((Q_BLOCK, LANES), lambda b, i, s: (b * nq + jnp.maximum(i - 1, 0), COL_SW_K)),
            pl.BlockSpec((Q_BLOCK, LANES), lambda b, i, s: (b * nq + i, COL_SW_K)),
            pl.BlockSpec((Q_BLOCK, LANES), lambda b, i, s: (b * nq + jnp.maximum(i - 1, 0), COL_SW_V)),
            pl.BlockSpec((Q_BLOCK, LANES), lambda b, i, s: (b * nq + i, COL_SW_V)),
            pl.BlockSpec((SWA_HEADS, Q_BLOCK, 2 * Q_BLOCK), lambda b, i, s: (0, 0, 0)),
        ],
        out_specs=pl.BlockSpec((Q_BLOCK, SWA_W), lambda b, i, s: (b * nq + i, 0)),
    )
    return pl.pallas_call(
        _swa_kernel,
        grid_spec=grid_spec,
        out_shape=jax.ShapeDtypeStruct((n, SWA_W), F32),
        compiler_params=pltpu.CompilerParams(dimension_semantics=("arbitrary", "arbitrary")),
        name="swa_attention",
    )(sinks, proj, proj, proj, proj, proj, proj, bias)


def _moba_kernel(far_ref, q_ref, k_ref, v_ref, btop_ref, o_ref, kmean_ref, rhs_ref, *, group):
    p = pl.program_id(1)
    cur = pl.program_id(2)
    seq = k_ref.shape[0]
    tq = q_ref.shape[0]
    nb = seq // MOBA_BLOCK
    top_w = 2 * MOBA_BLOCK
    far_w = group * MOBA_BLOCK
    lane = lax.broadcasted_iota(jnp.int32, (tq, LANES), 1)

    @pl.when(cur == 0)
    def _():
        blk = lax.broadcasted_iota(jnp.int32, (LANES, seq), 0)
        pos = lax.broadcasted_iota(jnp.int32, (LANES, seq), 1)
        lo = blk * MOBA_BLOCK
        avg = jnp.where((pos >= lo) & (pos < lo + MOBA_BLOCK), 1.0 / MOBA_BLOCK, 0.0).astype(BF16)
        kmean_ref[...] = _dot(avg, k_ref[...]).astype(BF16)
        key_blk = lax.broadcasted_iota(jnp.int32, (seq, LANES), 0) // MOBA_BLOCK
        blk_lane = lax.broadcasted_iota(jnp.int32, (seq, LANES), 1)
        rhs_ref[:, :LANES] = k_ref[...]
        rhs_ref[:, LANES:] = jnp.where(key_blk == blk_lane, 1.0, 0.0).astype(BF16)

    q = q_ref[...]
    kmean = kmean_ref[...]

    prev_start = pl.multiple_of(jnp.maximum(cur - 1, 0) * MOBA_BLOCK, MOBA_BLOCK)
    own_start = pl.multiple_of(cur * MOBA_BLOCK, MOBA_BLOCK)
    rhs_top = jnp.concatenate([rhs_ref[pl.ds(prev_start, MOBA_BLOCK), :], rhs_ref[pl.ds(own_start, MOBA_BLOCK), :]],
                              axis=0)
    v_top = jnp.concatenate([v_ref[pl.ds(prev_start, MOBA_BLOCK), :], v_ref[pl.ds(own_start, MOBA_BLOCK), :]], axis=0)
    jt = lax.broadcasted_iota(jnp.int32, (tq, top_w), 1)
    rt = lax.broadcasted_iota(jnp.int32, (tq, top_w), 0)
    top_ok = jnp.logical_and(jt < MOBA_BLOCK, cur >= 1) | ((jt >= MOBA_BLOCK) & (rt >= jt - MOBA_BLOCK))

    def update(state, s, vb):
        m, l, acc = state
        m_new = jnp.maximum(m, jnp.max(s, axis=-1, keepdims=True))
        alpha = jnp.exp(m - m_new)
        pexp = jnp.exp(s - m_new)
        l = alpha * l + jnp.sum(pexp, axis=-1, keepdims=True)
        acc = alpha * acc + _dot(pexp.astype(BF16), vb)
        return m_new, l, acc

    lhs_far, states = [], []
    for hh in range(2):
        hmask = (lane >= HEAD_DIM * hh) & (lane < HEAD_DIM * (hh + 1))
        qg = jnp.where(hmask, q, jnp.zeros_like(q))
        qm = qg * ATTN_SCALE
        gate = jnp.where(lane < cur, _dot_t(qg, kmean), NEG)
        sel = jnp.zeros((tq, LANES), jnp.bool_)
        for _ in range(min(MOBA_TOPK, nb)):
            mx = jnp.max(gate, axis=-1, keepdims=True)
            first = jnp.min(jnp.where(gate == mx, lane, LANES), axis=-1, keepdims=True)
            hit = lane == first
            sel = sel | hit
            gate = jnp.where(hit, -jnp.inf, gate)
        open_top = (sel & (lane == cur - 1)) | (lane == cur)
        open_far = sel & (lane < cur - 1)
        lhs_top = jnp.concatenate([qm, jnp.where(open_top, 0.0, NEG).astype(BF16)], axis=1)
        s = _dot_t(lhs_top, rhs_top) + btop_ref[hh]
        s = jnp.where(top_ok, s, NEG)
        m = jnp.max(s, axis=-1, keepdims=True)
        pexp = jnp.exp(s - m)
        l = jnp.sum(pexp, axis=-1, keepdims=True)
        acc = _dot(pexp.astype(BF16), v_top)
        lhs_far.append(jnp.concatenate([qm, jnp.where(open_far, 0.0, NEG).astype(BF16)], axis=1))
        states.append((m, l, acc))

    def far_group(g, states):
        start = pl.multiple_of(g * far_w, far_w)
        rhs = rhs_ref[pl.ds(start, far_w), :]
        vg = v_ref[pl.ds(start, far_w), :]
        return tuple(update(states[hh], _dot_t(lhs_far[hh], rhs) + far_ref[2 * p + hh], vg) for hh in range(2))

    n_far = jnp.maximum(cur - 1, 0)
    states = lax.fori_loop(0, (n_far + group - 1) // group, far_group, tuple(states))
    outs = [acc / l for (_, l, acc) in states]
    o_ref[...] = jnp.where(lane < HEAD_DIM, outs[0], outs[1])


def _moba_attention(proj, far_bias, bias_top, batch, seq):
    n = proj.shape[0]
    nq = seq // MOBA_BLOCK
    pairs = MOBA_HEADS // 2
    nb = seq // MOBA_BLOCK
    group = min(MOBA_GROUP, nb)
    assert nb % group == 0
    grid_spec = pltpu.PrefetchScalarGridSpec(
        num_scalar_prefetch=1,
        grid=(batch, pairs, nq),
        in_specs=[
            pl.BlockSpec((MOBA_BLOCK, LANES), lambda b, p, i, f: (b * nq + i, COL_MB_Q + p)),
            pl.BlockSpec((seq, LANES), lambda b, p, i, f: (b, COL_MB_K + p)),
            pl.BlockSpec((seq, LANES), lambda b, p, i, f: (b, COL_MB_V + p)),
            pl.BlockSpec((2, MOBA_BLOCK, 2 * MOBA_BLOCK), lambda b, p, i, f: (p, 0, 0)),
        ],
        out_specs=pl.BlockSpec((MOBA_BLOCK, LANES), lambda b, p, i, f: (b * nq + i, p)),
        scratch_shapes=[pltpu.VMEM((LANES, LANES), BF16), pltpu.VMEM((seq, 2 * LANES), BF16)],
    )
    return pl.pallas_call(
        functools.partial(_moba_kernel, group=group),
        grid_spec=grid_spec,
        out_shape=jax.ShapeDtypeStruct((n, MOBA_W), F32),
        compiler_params=pltpu.CompilerParams(
            dimension_semantics=("arbitrary", "arbitrary", "arbitrary"), vmem_limit_bytes=VMEM_LIMIT),
        name="moba_attention",
    )(far_bias, proj, proj, proj, bias_top)


def _outproj_kernel(osb_ref, osw_ref, omb_ref, x_ref, m_ref, gg_ref, wo_ref, gf_ref, wr_ref, br_ref,
                    xo_ref, h_ref, te_ref, tw_ref, cnt_ref):
    m = m_ref[0]

    @pl.when(pl.program_id(0) == 0)
    def _():
        cnt_ref[...] = jnp.zeros_like(cnt_ref)

    c1, c2 = SB_W, SB_W + SWA_W
    y = _dot(_rms(osb_ref[...], gg_ref[:, :c1]).astype(BF16), wo_ref[:c1, :])
    y = y + _dot(_rms(osw_ref[...], gg_ref[:, c1:c2]).astype(BF16), wo_ref[c1:c2, :])
    y = y + _dot(_rms(omb_ref[...], gg_ref[:, c2:]).astype(BF16), wo_ref[c2:, :])
    x = x_ref[...] + m[2:3] * y
    xo_ref[...] = x
    h = _rms(x, gf_ref[...]) * (1.0 + m[4:5]) + m[3:4]
    _to_row_tiles(h_ref, h)
    logits =_dot(h.astype(BF16), wr_ref[...]) + br_ref[...]
    lane = lax.broadcasted_iota(jnp.int32, logits.shape, 1)
    ids = jnp.zeros(logits.shape, jnp.int32)
    wts = jnp.zeros(logits.shape, F32)
    chosen = jnp.zeros(logits.shape, F32)
    top = None
    denom = None
    for r in range(TOP_K):
        mx = jnp.max(logits, axis=-1, keepdims=True)
        first = jnp.min(jnp.where(logits == mx, lane, LANES), axis=-1, keepdims=True)
        hit = lane == first
        logits = jnp.where(hit, -jnp.inf, logits)
        chosen = jnp.where(hit, 1.0, chosen)
        if r == 0:
            top = mx
        e = jnp.exp(mx - top)
        denom = e if r == 0 else denom + e
        ids = jnp.where(lane == r, first, ids)
        wts = jnp.where(lane == r, e, wts)
    te_ref[...] = ids
    tw_ref[...] = wts / denom
    cnt_ref[...] += _dot(jnp.ones((cnt_ref.shape[0], chosen.shape[0]), BF16), chosen.astype(BF16))


def _outproj_router(o_sb, o_sw, o_mb, x, mods, g_group, w_out_b, g_ffn, w_router_b, b_router_p, seq):
    n, d = x.shape
    tm = min(TOKEN_TILE, seq)
    per_batch = seq // tm
    row = lambda i: (i, 0)
    const = lambda i: (0, 0)
    return pl.pallas_call(
        _outproj_kernel,
        grid=(n // tm,),
        in_specs=[
            pl.BlockSpec((tm, SB_W), row),
            pl.BlockSpec((tm, SWA_W), row),
            pl.BlockSpec((tm, MOBA_W), row),
            pl.BlockSpec((tm, d), row),
            pl.BlockSpec((1, 6, d), lambda i: (i // per_batch, 0, 0)),
            pl.BlockSpec((1, d), const),
            pl.BlockSpec((d, d), const),
            pl.BlockSpec((1, d), const),
            pl.BlockSpec((d, LANES), const),
            pl.BlockSpec((1, LANES), const),
        ],
        out_specs=[
            pl.BlockSpec((tm, d), row),
            pl.BlockSpec((tm * SUBLANES, LANES), row),
            pl.BlockSpec((tm, LANES), row),
            pl.BlockSpec((tm, LANES), row),
            pl.BlockSpec((8, LANES), const),
        ],
        out_shape=[
            jax.ShapeDtypeStruct((n, d), F32),
            jax.ShapeDtypeStruct((n * SUBLANES, LANES), F32),
            jax.ShapeDtypeStruct((n, LANES), jnp.int32),
            jax.ShapeDtypeStruct((n, LANES), F32),
            jax.ShapeDtypeStruct((8, LANES), F32),
        ],
        compiler_params=pltpu.CompilerParams(dimension_semantics=("arbitrary",), vmem_limit_bytes=VMEM_LIMIT),
        name="outproj_router",
    )(o_sb, o_sw, o_mb, x, mods, g_group.reshape(1, d), w_out_b, g_ffn.reshape(1, d), w_router_b, b_router_p)


def _moe_kernel(be_ref, nu_ref, xs_ref, wgu_ref, bgu_ref, wd_ref, bd_ref, o_ref, wgu_b, wd_b):
    i = pl.program_id(0)
    d_ff = wd_b.shape[0]

    @pl.when(i < nu_ref[0])
    def _():
        changed = jnp.logical_or(i == 0, be_ref[i] != be_ref[jnp.maximum(i - 1, 0)])

        @pl.when(changed)
        def _():
            wgu_b[...] = wgu_ref[0, 0].astype(BF16)
            wd_b[...] = wd_ref[0, 0].astype(BF16)

        x = _from_row_tiles(xs_ref, MOE_ROWS).astype(BF16)
        gu = _dot(x, wgu_b[...]) + bgu_ref[0, 0]
        gate = jnp.minimum(gu[:, :d_ff], SWIGLU_LIMIT)
        up = jnp.clip(gu[:, d_ff:], -SWIGLU_LIMIT, SWIGLU_LIMIT)
        act = (up + 1.0) * gate * (1.0 / (1.0 + jnp.exp(-SWIGLU_ALPHA * gate)))
        _to_row_tiles(o_ref, _dot(act.astype(BF16), wd_b[...]) + bd_ref[0, 0])

    @pl.when(i >= nu_ref[0])
    def _():
        o_ref[...] = jnp.zeros_like(o_ref)


def _moe_experts(xs, block_e, n_used, layer, w_gate_up, b_gate_up, w_down, b_down):
    depth, n_exp, d, two_f = w_gate_up.shape
    n_rows = xs.shape[0] // SUBLANES
    d_ff = two_f // 2
    n_blocks = n_rows // MOE_ROWS
    blk = lambda i, be, nu: (jnp.minimum(i, nu[0] - 1), 0)
    grid_spec = pltpu.PrefetchScalarGridSpec(
        num_scalar_prefetch=2,
        grid=(n_blocks,),
        in_specs=[
            pl.BlockSpec((MOE_ROWS * SUBLANES, LANES), blk),
            pl.BlockSpec((1, 1, d, two_f), lambda i, be, nu: (layer, be[i], 0, 0)),
            pl.BlockSpec((1, 1, 1, two_f), lambda i, be, nu: (layer, be[i], 0, 0)),
            pl.BlockSpec((1, 1, d_ff, d), lambda i, be, nu: (layer, be[i], 0, 0)),
            pl.BlockSpec((1, 1, 1, d), lambda i, be, nu: (layer, be[i], 0, 0)),
        ],
        out_specs=pl.BlockSpec((MOE_ROWS * SUBLANES, LANES), lambda i, be, nu: (i, 0)),
        scratch_shapes=[
            pltpu.VMEM((d, two_f), BF16),
            pltpu.VMEM((d_ff, d), BF16),
        ],
    )
    return pl.pallas_call(
        _moe_kernel,
        grid_spec=grid_spec,
        out_shape=jax.ShapeDtypeStruct((n_rows * SUBLANES, LANES), F32),
        compiler_params=pltpu.CompilerParams(dimension_semantics=("arbitrary",), vmem_limit_bytes=VMEM_LIMIT),
        name="moe_experts",
    )(block_e, n_used, xs, w_gate_up, b_gate_up.reshape(depth, n_exp, 1, two_f), w_down,
      b_down.reshape(depth, n_exp, 1, d))


def _combine_kernel(pos_ref, pos_next_ref, ys_hbm, x_ref, tw_ref, m_ref, gfin_ref, o_ref, buf, pos_smem,
                    sem_rows, sem_pos, *, final):
    i = pl.program_id(0)
    n_steps = pl.num_programs(0)
    tm = x_ref.shape[0]
    slot = i % 2

    def row_copy(s, t, k, row):
        src = pl.multiple_of(row * SUBLANES, SUBLANES)
        dst = pl.multiple_of(t * SUBLANES, SUBLANES)
        return pltpu.make_async_copy(ys_hbm.at[pl.ds(src, SUBLANES)], buf.at[s, k, pl.ds(dst, SUBLANES)],
                                     sem_rows.at[s])

    def for_row_groups(s, act):
        def body(g, carry):
            t0 = g * ROW_GROUP
            rows = [[pos_smem[s, t0 + j, k] for k in range(TOP_K)] for j in range(ROW_GROUP)]
            for j in range(ROW_GROUP):
                for k in range(TOP_K):
                    act(row_copy(s, t0 + j, k, rows[j][k]))
            return carry

        lax.fori_loop(0, tm // ROW_GROUP, body, 0)

    def start_gather(src_ref, s):
        to_smem = pltpu.make_async_copy(src_ref, pos_smem.at[s], sem_pos)
        to_smem.start()
        to_smem.wait()
        for_row_groups(s, lambda cp: cp.start())

    @pl.when(i == 0)
    def _():
        start_gather(pos_ref, 0)

    @pl.when(i + 1 < n_steps)
    def _():
        start_gather(pos_next_ref, 1 - slot)

    for_row_groups(slot, lambda cp: cp.wait())

    tw = tw_ref[...]
    cols = []
    for s in range(SUBLANES):
        acc = tw[:, 0:1] * buf.at[slot, 0][pl.ds(s, tm, stride=SUBLANES), :]
        for k in range(1, TOP_K):
            acc = acc + tw[:, k:k + 1] * buf.at[slot, k][pl.ds(s, tm, stride=SUBLANES), :]
        cols.append(acc)
    x = x_ref[...] + m_ref[0][5:6] * jnp.concatenate(cols, axis=1)
    if final:
        x = _rms(x, gfin_ref[...])
    o_ref[...] = x


def _moe_combine(ys, pos, x, tw, mods, g_final, seq, final):
    n, d = x.shape
    tm = min(COMBINE_TILE, seq)
    per_batch = seq // tm
    n_tiles = n // tm
    return pl.pallas_call(
        functools.partial(_combine_kernel, final=final),
        grid=(n_tiles,),
        in_specs=[
            pl.BlockSpec((tm, LANES), lambda i: (i, 0)),
            pl.BlockSpec((tm, LANES), lambda i: (jnp.minimum(i + 1, n_tiles - 1), 0)),
            pl.BlockSpec(memory_space=pl.ANY),
            pl.BlockSpec((tm, d), lambda i: (i, 0)),
            pl.BlockSpec((tm, LANES), lambda i: (i, 0)),
            pl.BlockSpec((1, 6, d), lambda i: (i // per_batch, 0, 0)),
            pl.BlockSpec((1, d), lambda i: (0, 0)),
        ],
        out_specs=pl.BlockSpec((tm, d), lambda i: (i, 0)),
        out_shape=jax.ShapeDtypeStruct((n, d), F32),
        scratch_shapes=[pltpu.VMEM((2, TOP_K, tm * SUBLANES, LANES), F32), pltpu.SMEM((2, tm, LANES), jnp.int32),
                        pltpu.SemaphoreType.DMA((2,)), pltpu.SemaphoreType.DMA],
        compiler_params=pltpu.CompilerParams(dimension_semantics=("arbitrary",), vmem_limit_bytes=VMEM_LIMIT),
        name="moe_combine",
    )(pos, pos, ys, x, tw, mods, g_final.reshape(1, d))


def _row_plan(counts, n_tok):
    cnt = counts[0, :N_EXPERTS].astype(jnp.int32)
    padded = (cnt + MOE_ROWS - 1) // MOE_ROWS * MOE_ROWS
    pad_end = jnp.cumsum(padded)
    pad_start = pad_end - padded
    n_blocks = -(-n_tok * TOP_K // MOE_ROWS) + N_EXPERTS
    n_used = (pad_end[-1] // MOE_ROWS).astype(jnp.int32)
    blk_first = jnp.minimum(jnp.arange(n_blocks, dtype=jnp.int32), n_used - 1) * MOE_ROWS
    block_e = jnp.sum(blk_first[:, None] >= pad_end[None, :], axis=1).astype(jnp.int32)
    pad_end0 = jnp.concatenate([jnp.zeros((1,), jnp.int32), pad_end.astype(jnp.int32)])
    pad_start_row = jnp.zeros((1, LANES), F32).at[0, :N_EXPERTS].set(pad_start.astype(F32))
    return pad_end0, pad_start_row, block_e, n_used.reshape(1), n_blocks * MOE_ROWS


def _dispatch_kernel(pend_ref, te_ref, h_hbm, pstart_ref, pos_ref, xs_hbm, carry_ref, zero_ref, hbuf, pos_smem,
                     sem_rows, sem_load, sem_misc):
    i = pl.program_id(0)
    n_steps = pl.num_programs(0)
    tm = te_ref.shape[0]
    blk = MOE_ROWS * SUBLANES
    tile = tm * SUBLANES
    lane = lax.broadcasted_iota(jnp.int32, (tm, LANES), 1)

    def tile_load(step):
        src = pl.multiple_of(step * tile, tile)
        return pltpu.make_async_copy(h_hbm.at[pl.ds(src, tile)], hbuf.at[step % 3], sem_load.at[step % 3])

    @pl.when(i == 0)
    def _():
        tile_load(0).start()

    @pl.when(i + 1 < n_steps)
    def _():
        tile_load(i + 1).start()

    @pl.when(i == 0)
    def _():
        carry_ref[...] = jnp.zeros_like(carry_ref)
        zero_ref[...] = jnp.zeros_like(zero_ref)

        def tail_copy(e):
            end = pl.multiple_of(pend_ref[e + 1] * SUBLANES, blk)
            return pltpu.make_async_copy(zero_ref, xs_hbm.at[pl.ds(end - blk, blk)], sem_misc)

        for e in range(N_EXPERTS):
            @pl.when(pend_ref[e + 1] > pend_ref[e])
            def _():
                tail_copy(e).start()
        for e in range(N_EXPERTS):
            @pl.when(pend_ref[e + 1] > pend_ref[e])
            def _():
                tail_copy(e).wait()

        def spare_copy(b):
            return pltpu.make_async_copy(zero_ref, xs_hbm.at[pl.ds(pl.multiple_of(b * blk, blk), blk)], sem_misc)

        first_spare = pend_ref[N_EXPERTS] // MOE_ROWS
        n_blocks = xs_hbm.shape[0] // blk

        def start_spare(b, carry):
            spare_copy(b).start()
            return carry

        def wait_spare(b, carry):
            spare_copy(b).wait()
            return carry

        lax.fori_loop(first_spare, n_blocks, start_spare, 0)
        lax.fori_loop(first_spare, n_blocks, wait_spare, 0)

    te = te_ref[...]
    hits = [lane == te[:, k:k + 1] for k in range(TOP_K)]
    cnt = jnp.zeros((tm, LANES), F32)
    for k in range(TOP_K):
        cnt = cnt + jnp.where(hits[k], 1.0, 0.0)
    cnt_b = cnt.astype(BF16)
    r = lax.broadcasted_iota(jnp.int32, (tm, tm), 0)
    c = lax.broadcasted_iota(jnp.int32, (tm, tm), 1)
    before = jnp.where(c < r, 1.0, 0.0).astype(BF16)
    base = _dot(before, cnt_b) + (carry_ref[0:1, :] + pstart_ref[...])
    pos = jnp.zeros((tm, LANES), jnp.int32)
    for k in range(TOP_K):
        pk = jnp.sum(jnp.where(hits[k], base, 0.0), axis=-1, keepdims=True).astype(jnp.int32)
        pos = jnp.where(lane == k, pk, pos)
    pos_ref[...] = pos
    carry_ref[...] = carry_ref[...] + _dot(jnp.ones((carry_ref.shape[0], tm), BF16), cnt_b)

    slot = i % 2
    to_smem = pltpu.make_async_copy(pos_ref, pos_smem.at[slot], sem_misc)
    to_smem.start()
    to_smem.wait()

    def row_copy(step, s, t, row):
        src = pl.multiple_of(t * SUBLANES, SUBLANES)
        dst = pl.multiple_of(row * SUBLANES, SUBLANES)
        return pltpu.make_async_copy(hbuf.at[step % 3, pl.ds(src, SUBLANES)], xs_hbm.at[pl.ds(dst, SUBLANES)],
                                     sem_rows.at[s])

    def for_row_groups(step, s, act):
        def body(g, carry):
            t0 = g * ROW_GROUP
            rows = [[pos_smem[s, t0 + j, k] for k in range(TOP_K)] for j in range(ROW_GROUP)]
            for j in range(ROW_GROUP):
                for k in range(TOP_K):
                    act(row_copy(step, s, t0 + j, rows[j][k]))
            return carry

        lax.fori_loop(0, tm // ROW_GROUP, body, 0)

    tile_load(i).wait()
    for_row_groups(i, slot, lambda cp: cp.start())

    def drain(step, s):
        for_row_groups(step, s, lambda cp: cp.wait())

    @pl.when(i >= 1)
    def _():
        drain(i - 1, 1 - slot)

    @pl.when(i == n_steps - 1)
    def _():
        drain(i, slot)


def _moe_dispatch(top_e, h, pad_end0, pad_start_row, n_rows, seq):
    n = top_e.shape[0]
    tm = min(DISPATCH_TILE, seq)
    grid_spec = pltpu.PrefetchScalarGridSpec(
        num_scalar_prefetch=1,
        grid=(n // tm,),
        in_specs=[
            pl.BlockSpec((tm, LANES), lambda i, pe: (i, 0)),
            pl.BlockSpec(memory_space=pl.ANY),
            pl.BlockSpec((1, LANES), lambda i, pe: (0, 0)),
        ],
        out_specs=[
            pl.BlockSpec((tm, LANES), lambda i, pe: (i, 0)),
            pl.BlockSpec(memory_space=pl.ANY),
        ],
        scratch_shapes=[
            pltpu.VMEM((8, LANES), F32),
            pltpu.VMEM((MOE_ROWS * SUBLANES, LANES), F32),
            pltpu.VMEM((3, tm * SUBLANES, LANES), F32),
            pltpu.SMEM((2, tm, LANES), jnp.int32),
            pltpu.SemaphoreType.DMA((2,)),
            pltpu.SemaphoreType.DMA((3,)),
            pltpu.SemaphoreType.DMA,
        ],
    )
    return pl.pallas_call(
        _dispatch_kernel,
        grid_spec=grid_spec,
        out_shape=[jax.ShapeDtypeStruct((n, LANES), jnp.int32),
                   jax.ShapeDtypeStruct((n_rows * SUBLANES, LANES), F32)],
        compiler_params=pltpu.CompilerParams(dimension_semantics=("arbitrary",), vmem_limit_bytes=VMEM_LIMIT),
        name="moe_dispatch",
    )(pad_end0, top_e, h, pad_start_row)


def _t5_bucket(dist):
    n = jnp.maximum(dist, 0)
    max_exact = NUM_BUCKETS // 2
    nf = jnp.maximum(n, 1).astype(F32)
    large = max_exact + (jnp.log(nf / max_exact) / math.log(MAX_DISTANCE / max_exact)
                         * (NUM_BUCKETS - max_exact)).astype(jnp.int32)
    large = jnp.minimum(large, NUM_BUCKETS - 1)
    return jnp.where(n < max_exact, n, large)


def _bias_lookup(table, dist):
    onehot = (_t5_bucket(dist)[..., None] == jnp.arange(NUM_BUCKETS)).astype(F32)
    out = jnp.einsum("...k,kh->h...", onehot, table.astype(F32), precision=lax.Precision.HIGHEST)
    return out.astype(F32)


def _bias_tables(rel_bias):
    r = jnp.arange(Q_BLOCK)[:, None]
    j2 = jnp.arange(2 * Q_BLOCK)[None, :]
    swa = _bias_lookup(rel_bias[:, :SWA_HEADS], Q_BLOCK + r - j2)
    tab = rel_bias[:, SWA_HEADS:]
    rb = jnp.arange(MOBA_BLOCK)[:, None]
    jb = jnp.arange(2 * MOBA_BLOCK)[None, :]
    top = _bias_lookup(tab, rb + MOBA_BLOCK - jb)
    far = tab[NUM_BUCKETS - 1]
    return swa, top, far.astype(F32)


def kernel(x, c, w_in, w_out, g_norm_mix, g_norm_ffn, g_group, w_mod, b_mod, swa_sinks, rel_bias,
           w_router, b_router, w_gate_up, b_gate_up, w_down, b_down, g_final):
    batch, seq, d = x.shape
    depth = w_in.shape[0]
    n = batch * seq
    assert seq % MOBA_BLOCK == 0 and seq // MOBA_BLOCK <= LANES and d == SUBLANES * LANES

    mods = _modulation(c, w_mod, b_mod)
    bias_swa, bias_top, bias_far = _bias_tables(rel_bias)
    w_in_b = w_in.astype(BF16)
    w_out_b = w_out.astype(BF16)
    w_router_b = jnp.zeros((depth, d, LANES), BF16).at[:, :, :N_EXPERTS].set(w_router.astype(BF16))
    b_router_p = jnp.full((depth, 1, LANES), NEG, F32).at[:, 0, :N_EXPERTS].set(b_router)

    xf = x.reshape(n, d)
    for l in range(depth):
        proj = _inproj(xf, g_norm_mix[l], mods[l], w_in_b[l], seq)
        o_sb = _sb_attention(proj, batch, seq)
        o_sw = _swa_attention(proj, swa_sinks[l], bias_swa, batch, seq)
        o_mb = _moba_attention(proj, bias_far, bias_top, batch, seq)
        xf, h, top_e, top_w, counts = _outproj_router(o_sb, o_sw, o_mb, xf, mods[l], g_group[l], w_out_b[l],
                                                      g_norm_ffn[l], w_router_b[l], b_router_p[l], seq)
        pad_end0, pad_start_row, block_e, n_used, n_rows = _row_plan(counts, n)
        pos, xs = _moe_dispatch(top_e, h, pad_end0, pad_start_row, n_rows, seq)
        ys = _moe_experts(xs, block_e, n_used, l, w_gate_up, b_gate_up, w_down, b_down)
        xf = _moe_combine(ys, pos, xf, top_w, mods[l], g_final, seq, final=(l == depth - 1))
    return xf.reshape(batch, seq, d)
```

```python
import functools
import math

import jax
import jax.numpy as jnp
from jax import lax
from jax.experimental import pallas as pl
from jax.experimental.pallas import tpu as pltpu

F32 = jnp.float32
BF16 = jnp.bfloat16

HEAD_DIM = 64
SB_HEADS = 4
SWA_HEADS = 8
SWA_KV_HEADS = 2
MOBA_HEADS = 4
SB_W = SB_HEADS * HEAD_DIM
SWA_W = SWA_HEADS * HEAD_DIM
SWA_KV_W = SWA_KV_HEADS * HEAD_DIM
MOBA_W = MOBA_HEADS * HEAD_DIM
Q_BLOCK = 128
WINDOW = 128
MOBA_BLOCK = 256
MOBA_TOPK = 3
NUM_BUCKETS = 32
MAX_DISTANCE = 128
N_EXPERTS = 32
TOP_K = 4
SWIGLU_LIMIT = 7.0
SWIGLU_ALPHA = 1.702
MOE_BLOCK = 256
EPS = 1e-6
NEG = -1e30
ATTN_SCALE = HEAD_DIM ** -0.5

LANES = 128
SUBLANES = 8
COL_SB_Q, COL_SB_K, COL_SB_V = 0, 2, 4
COL_SW_Q, COL_SW_K, COL_SW_V = 6, 10, 11
COL_MB_Q, COL_MB_K, COL_MB_V = 12, 14, 16
IN_W = 18 * LANES

SB_SPAN = 512
SB_QUERIES = 256
MOBA_GROUP = 4
TOKEN_TILE = 512
MOE_ROWS = 512
DISPATCH_TILE = 256
COMBINE_TILE = 256
ROW_GROUP = 8
DMA_PRIORITIES = 2
VMEM_LIMIT = 56 * 1024 * 1024


def _rms(x, g):
    return x * lax.rsqrt(jnp.mean(x * x, axis=-1, keepdims=True) + EPS) * g


def _dot_t(a, b):
    return lax.dot_general(a, b, (((1,), (1,)), ((), ())), preferred_element_type=F32)


def _dot(a, b):
    return jnp.dot(a, b, preferred_element_type=F32)


def _to_row_tiles(ref, x):
    rows = x.shape[0]
    for s in range(SUBLANES):
        ref[pl.ds(s, rows, stride=SUBLANES), :] = x[:, s * LANES:(s + 1) * LANES]


def _from_row_tiles(ref, rows):
    return jnp.concatenate([ref[pl.ds(s, rows, stride=SUBLANES), :] for s in range(SUBLANES)], axis=1)


def _mod_kernel(c_ref, w_ref, b_ref, o_ref):
    c = c_ref[...]
    ca = c * (1.0 / (1.0 + jnp.exp(-c)))
    o_ref[0] = _dot(ca, w_ref[0]) + b_ref[0]


def _modulation(c, w_mod, b_mod):
    depth, d, six_d = w_mod.shape
    b = c.shape[0]
    rows = 8
    c8 = jnp.zeros((rows, d), F32).at[:b].set(c)
    tn = six_d // 6
    out = pl.pallas_call(
        _mod_kernel,
        grid=(depth, six_d // tn),
        in_specs=[
            pl.BlockSpec((rows, d), lambda l, j: (0, 0)),
            pl.BlockSpec((1, d, tn), lambda l, j: (l, 0, j)),
            pl.BlockSpec((1, 1, tn), lambda l, j: (l, 0, j)),
        ],
        out_specs=pl.BlockSpec((1, rows, tn), lambda l, j: (l, 0, j)),
        out_shape=jax.ShapeDtypeStruct((depth, rows, six_d), F32),
        name="adaln_mod",
    )(c8, w_mod, b_mod.reshape(depth, 1, six_d))
    return out[:, :b].reshape(depth, b, 6, d)


def _inproj_kernel(x_ref, g_ref, m_ref, w_ref, o_ref):
    m = m_ref[0]
    h = _rms(x_ref[...], g_ref[...]) * (1.0 + m[1:2]) + m[0:1]
    hb = h.astype(BF16)
    step = 2 * LANES
    for j in range(IN_W // step):
        o_ref[:, j * step:(j + 1) * step] = _dot(hb, w_ref[:, j * step:(j + 1) * step]).astype(BF16)


def _inproj(x, g, mods, w_in_b, seq):
    n, d = x.shape
    tm = min(TOKEN_TILE, seq)
    per_batch = seq // tm
    return pl.pallas_call(
        _inproj_kernel,
        grid=(n // tm,),
        in_specs=[
            pl.BlockSpec((tm, d), lambda i: (i, 0)),
            pl.BlockSpec((1, d), lambda i: (0, 0)),
            pl.BlockSpec((1, 6, d), lambda i: (i // per_batch, 0, 0)),
            pl.BlockSpec((d, IN_W), lambda i: (0, 0)),
        ],
        out_specs=pl.BlockSpec((tm, IN_W), lambda i: (i, 0)),
        out_shape=jax.ShapeDtypeStruct((n, IN_W), BF16),
        compiler_params=pltpu.CompilerParams(dimension_semantics=("arbitrary",), vmem_limit_bytes=VMEM_LIMIT),
        name="norm_inproj",
    )(x, g.reshape(1, d), mods, w_in_b)


def _sb_kernel(q_ref, k_ref, v_ref, o_ref, *, span):
    i = pl.program_id(2)
    tq = q_ref.shape[0]
    n_chunks = span // LANES
    lane = lax.broadcasted_iota(jnp.int32, (tq, LANES), 1)
    tri_r = lax.broadcasted_iota(jnp.int32, (LANES, LANES), 0)
    tri_c = lax.broadcasted_iota(jnp.int32, (LANES, LANES), 1)
    tri = jnp.where(tri_r > tri_c, 1.0, 0.0).astype(BF16)
    q = q_ref[...] * ATTN_SCALE
    qms = [jnp.where((lane >= HEAD_DIM * hh) & (lane < HEAD_DIM * (hh + 1)), q, jnp.zeros_like(q))
           for hh in range(2)]
    q_start = i * tq
    key_off = lax.broadcasted_iota(jnp.int32, (tq, span), 1) - lax.broadcasted_iota(jnp.int32, (tq, span), 0)

    def do_span(sidx, state, masked):
        start = pl.multiple_of(sidx * span, span)
        ks = k_ref[pl.ds(start, span), :]
        vs = v_ref[pl.ds(start, span), :]
        new_state = []
        for hh in range(2):
            run, acc = state[hh]
            z = _dot_t(qms[hh], ks)
            sp = jnp.maximum(z, 0.0) + jnp.log(1.0 + jnp.exp(-jnp.abs(z)))
            log_1m = -sp
            log_beta = z - sp
            parts = [None] * n_chunks
            for c in reversed(range(n_chunks)):
                cols = slice(c * LANES, (c + 1) * LANES)
                lc = log_1m[:, cols]
                if masked:
                    strict = (key_off[:, cols] + (start - q_start)) < 0
                    lc = jnp.where(strict, lc, 0.0)
                later = _dot(lc.astype(BF16), tri) + run
                a = jnp.exp(log_beta[:, cols] + later)
                if masked:
                    a = jnp.where(strict, a, 0.0)
                parts[c] = a.astype(BF16)
                run = run + jnp.sum(lc, axis=-1, keepdims=True)
            acc = acc + _dot(jnp.concatenate(parts, axis=1), vs)
            new_state.append((run, acc))
        return tuple(new_state)

    init = ((jnp.zeros((tq, 1), F32), jnp.zeros((tq, LANES), F32)),) * 2
    top = q_start // span
    state = do_span(top, init, True)
    state = lax.fori_loop(0, top, lambda jj, st: do_span(top - 1 - jj, st, False), state)
    o_ref[...] = jnp.where(lane < HEAD_DIM, state[0][1], state[1][1])


def _sb_attention(proj, batch, seq):
    n = proj.shape[0]
    span = min(SB_SPAN, seq)
    tq = min(SB_QUERIES, span)
    assert span % tq == 0 and seq % span == 0
    nq = seq // tq
    pairs = SB_HEADS // 2
    return pl.pallas_call(
        functools.partial(_sb_kernel, span=span),
        grid=(batch, pairs, nq),
        in_specs=[
            pl.BlockSpec((tq, LANES), lambda b, p, i: (b * nq + i, COL_SB_Q + p)),
            pl.BlockSpec((seq, LANES), lambda b, p, i: (b, COL_SB_K + p)),
            pl.BlockSpec((seq, LANES), lambda b, p, i: (b, COL_SB_V + p)),
        ],
        out_specs=pl.BlockSpec((tq, LANES), lambda b, p, i: (b * nq + i, p)),
        out_shape=jax.ShapeDtypeStruct((n, SB_W), F32),
        compiler_params=pltpu.CompilerParams(
            dimension_semantics=("arbitrary", "arbitrary", "arbitrary"), vmem_limit_bytes=VMEM_LIMIT),
        name="sb_attention",
    )(proj, proj, proj)


def _swa_kernel(sink_ref, qa_ref, qb_ref, kp_ref, kc_ref, vp_ref, vc_ref, bias_ref, o_ref):
    i = pl.program_id(1)
    kk = jnp.concatenate([kp_ref[...], kc_ref[...]], axis=0)
    vv = jnp.concatenate([vp_ref[...], vc_ref[...]], axis=0)
    r = lax.broadcasted_iota(jnp.int32, (Q_BLOCK, 2 * Q_BLOCK), 0)
    j = lax.broadcasted_iota(jnp.int32, (Q_BLOCK, 2 * Q_BLOCK), 1)
    dist = Q_BLOCK + r - j
    valid = (dist >= 0) & (dist < WINDOW) & ((j >= Q_BLOCK) | (i > 0))
    group = SWA_HEADS // SWA_KV_HEADS
    for h in range(SWA_HEADS):
        g = h // group
        q_ref = qa_ref if h < group else qb_ref
        c0 = HEAD_DIM * (h % group)
        qh = q_ref[:, c0:c0 + HEAD_DIM] * ATTN_SCALE
        s = _dot_t(qh, kk[:, HEAD_DIM * g:HEAD_DIM * (g + 1)]) + bias_ref[h]
        s = jnp.where(valid, s, NEG)
        sink = sink_ref[h]
        m = jnp.maximum(jnp.max(s, axis=-1, keepdims=True), sink)
        p = jnp.exp(s - m)
        l = jnp.sum(p, axis=-1, keepdims=True) + jnp.exp(sink - m)
        o = _dot(p.astype(BF16), vv[:, HEAD_DIM * g:HEAD_DIM * (g + 1)])
        o_ref[:, HEAD_DIM * h:HEAD_DIM * (h + 1)] = o / l


def _swa_attention(proj, sinks, bias, batch, seq):
    n = proj.shape[0]
    nq = seq // Q_BLOCK
    wide = 2 * LANES
    grid_spec = pltpu.PrefetchScalarGridSpec(
        num_scalar_prefetch=1,
        grid=(batch, nq),
        in_specs=[
            pl.BlockSpec((Q_BLOCK, wide), lambda b, i, s: (b * nq + i, COL_SW_Q // 2)),
            pl.BlockSpec((Q_BLOCK, wide), lambda b, i, s: (b * nq + i, COL_SW_Q // 2 + 1)),
            pl.BlockSpec((Q_BLOCK, LANES), lambda b, i, s: (b * nq + jnp.maximum(i - 1, 0), COL_SW_K)),
            pl.BlockSpec((Q_BLOCK, LANES), lambda b, i, s: (b * nq + i, COL_SW_K)),
            pl.BlockSpec((Q_BLOCK, LANES), lambda b, i, s: (b * nq + jnp.maximum(i - 1, 0), COL_SW_V)),
            pl.BlockSpec((Q_BLOCK, LANES), lambda b, i, s: (b * nq + i, COL_SW_V)),
            pl.BlockSpec((SWA_HEADS, Q_BLOCK, 2 * Q_BLOCK), lambda b, i, s: (0, 0, 0)),
        ],
        out_specs=pl.BlockSpec((Q_BLOCK, SWA_W), lambda b, i, s: (b * nq + i, 0)),
    )
    return pl.pallas_call(
        _swa_kernel,
        grid_spec=grid_spec,
        out_shape=jax.ShapeDtypeStruct((n, SWA_W), F32),
        compiler_params=pltpu.CompilerParams(dimension_semantics=("arbitrary", "arbitrary")),
        name="swa_attention",
    )(sinks, proj, proj, proj, proj, proj, proj, bias)


def _moba_kernel(far_ref, q_ref, k_ref, v_ref, btop_ref, o_ref, kmean_ref, rhs_ref, *, group):
    p = pl.program_id(1)
    cur = pl.program_id(2)
    seq = k_ref.shape[0]
    tq = q_ref.shape[0]
    nb = seq // MOBA_BLOCK
    top_w = 2 * MOBA_BLOCK
    far_w = group * MOBA_BLOCK
    lane = lax.broadcasted_iota(jnp.int32, (tq, LANES), 1)

    @pl.when(cur == 0)
    def _():
        blk = lax.broadcasted_iota(jnp.int32, (LANES, seq), 0)
        pos = lax.broadcasted_iota(jnp.int32, (LANES, seq), 1)
        lo = blk * MOBA_BLOCK
        avg = jnp.where((pos >= lo) & (pos < lo + MOBA_BLOCK), 1.0 / MOBA_BLOCK, 0.0).astype(BF16)
        kmean_ref[...] = _dot(avg, k_ref[...]).astype(BF16)
        key_blk = lax.broadcasted_iota(jnp.int32, (seq, LANES), 0) // MOBA_BLOCK
        blk_lane = lax.broadcasted_iota(jnp.int32, (seq, LANES), 1)
        rhs_ref[:, :LANES] = k_ref[...]
        rhs_ref[:, LANES:] = jnp.where(key_blk == blk_lane, 1.0, 0.0).astype(BF16)

    q = q_ref[...]
    kmean = kmean_ref[...]
    lane_f = lane.astype(F32)

    prev_start = pl.multiple_of(jnp.maximum(cur - 1, 0) * MOBA_BLOCK, MOBA_BLOCK)
    own_start = pl.multiple_of(cur * MOBA_BLOCK, MOBA_BLOCK)
    rhs_top = jnp.concatenate([rhs_ref[pl.ds(prev_start, MOBA_BLOCK), :], rhs_ref[pl.ds(own_start, MOBA_BLOCK), :]],
                              axis=0)
    v_top = jnp.concatenate([v_ref[pl.ds(prev_start, MOBA_BLOCK), :], v_ref[pl.ds(own_start, MOBA_BLOCK), :]], axis=0)
    jt = lax.broadcasted_iota(jnp.int32, (tq, top_w), 1)
    rt = lax.broadcasted_iota(jnp.int32, (tq, top_w), 0)
    top_ok = jnp.logical_and(jt < MOBA_BLOCK, cur >= 1) | ((jt >= MOBA_BLOCK) & (rt >= jt - MOBA_BLOCK))

    def update(state, s, vb):
        m, l, acc = state
        m_new = jnp.maximum(m, jnp.max(s, axis=-1, keepdims=True))
        alpha = jnp.exp(m - m_new)
        pexp = jnp.exp(s - m_new)
        l = alpha * l + jnp.sum(pexp, axis=-1, keepdims=True)
        acc = alpha * acc + _dot(pexp.astype(BF16), vb)
        return m_new, l, acc

    lhs_far, states = [], []
    for hh in range(2):
        hmask = (lane >= HEAD_DIM * hh) & (lane < HEAD_DIM * (hh + 1))
        qg = jnp.where(hmask, q, jnp.zeros_like(q))
        qm = qg * ATTN_SCALE
        gate = jnp.where(lane < cur, _dot_t(qg, kmean), NEG)
        sel = jnp.zeros((tq, LANES), jnp.bool_)
        for _ in range(min(MOBA_TOPK, nb)):
            mx = jnp.max(gate, axis=-1, keepdims=True)
            first = jnp.min(jnp.where(gate == mx, lane_f, float(LANES)), axis=-1, keepdims=True)
            hit = lane_f == first
            sel = sel | hit
            gate = jnp.where(hit, -jnp.inf, gate)
        open_top = (sel & (lane == cur - 1)) | (lane == cur)
        open_far = sel & (lane < cur - 1)
        lhs_top = jnp.concatenate([qm, jnp.where(open_top, 0.0, NEG).astype(BF16)], axis=1)
        s = _dot_t(lhs_top, rhs_top) + btop_ref[hh]
        s = jnp.where(top_ok, s, NEG)
        m = jnp.max(s, axis=-1, keepdims=True)
        pexp = jnp.exp(s - m)
        l = jnp.sum(pexp, axis=-1, keepdims=True)
        acc = _dot(pexp.astype(BF16), v_top)
        lhs_far.append(jnp.concatenate([qm, jnp.where(open_far, 0.0, NEG).astype(BF16)], axis=1))
        states.append((m, l, acc))

    def far_group(g, states):
        start = pl.multiple_of(g * far_w, far_w)
        rhs = rhs_ref[pl.ds(start, far_w), :]
        vg = v_ref[pl.ds(start, far_w), :]
        return tuple(update(states[hh], _dot_t(lhs_far[hh], rhs) + far_ref[2 * p + hh], vg) for hh in range(2))

    n_far = jnp.maximum(cur - 1, 0)
    states = lax.fori_loop(0, (n_far + group - 1) // group, far_group, tuple(states))
    outs = [acc / l for (_, l, acc) in states]
    o_ref[...] = jnp.where(lane < HEAD_DIM, outs[0], outs[1])


def _moba_attention(proj, far_bias, bias_top, batch, seq):
    n = proj.shape[0]
    nq = seq // MOBA_BLOCK
    pairs = MOBA_HEADS // 2
    nb = seq // MOBA_BLOCK
    group = min(MOBA_GROUP, nb)
    assert nb % group == 0
    grid_spec = pltpu.PrefetchScalarGridSpec(
        num_scalar_prefetch=1,
        grid=(batch, pairs, nq),
        in_specs=[
            pl.BlockSpec((MOBA_BLOCK, LANES), lambda b, p, i, f: (b * nq + i, COL_MB_Q + p)),
            pl.BlockSpec((seq, LANES), lambda b, p, i, f: (b, COL_MB_K + p)),
            pl.BlockSpec((seq, LANES), lambda b, p, i, f: (b, COL_MB_V + p)),
            pl.BlockSpec((2, MOBA_BLOCK, 2 * MOBA_BLOCK), lambda b, p, i, f: (p, 0, 0)),
        ],
        out_specs=pl.BlockSpec((MOBA_BLOCK, LANES), lambda b, p, i, f: (b * nq + i, p)),
        scratch_shapes=[pltpu.VMEM((LANES, LANES), BF16), pltpu.VMEM((seq, 2 * LANES), BF16)],
    )
    return pl.pallas_call(
        functools.partial(_moba_kernel, group=group),
        grid_spec=grid_spec,
        out_shape=jax.ShapeDtypeStruct((n, MOBA_W), F32),
        compiler_params=pltpu.CompilerParams(
            dimension_semantics=("arbitrary", "arbitrary", "arbitrary"), vmem_limit_bytes=VMEM_LIMIT),
        name="moba_attention",
    )(far_bias, proj, proj, proj, bias_top)


def _outproj_kernel(osb_ref, osw_ref, omb_ref, x_ref, m_ref, gg_ref, wo_ref, gf_ref, wr_ref, br_ref,
                    xo_ref, h_ref, te_ref, tw_ref, cnt_ref):
    m = m_ref[0]

    @pl.when(pl.program_id(0) == 0)
    def _():
        cnt_ref[...] = jnp.zeros_like(cnt_ref)

    c1, c2 = SB_W, SB_W + SWA_W
    y = _dot(_rms(osb_ref[...], gg_ref[:, :c1]).astype(BF16), wo_ref[:c1, :])
    y = y + _dot(_rms(osw_ref[...], gg_ref[:, c1:c2]).astype(BF16), wo_ref[c1:c2, :])
    y = y + _dot(_rms(omb_ref[...], gg_ref[:, c2:]).astype(BF16), wo_ref[c2:, :])
    x = x_ref[...] + m[2:3] * y
    xo_ref[...] = x
    h = _rms(x, gf_ref[...]) * (1.0 + m[4:5]) + m[3:4]
    _to_row_tiles(h_ref, h)
    logits = _dot(h.astype(BF16), wr_ref[...]) + br_ref[...]
    lane = lax.broadcasted_iota(jnp.int32, logits.shape, 1)
    lane_f = lane.astype(F32)
    ids = jnp.zeros(logits.shape, F32)
    wts = jnp.zeros(logits.shape, F32)
    chosen = jnp.zeros(logits.shape, F32)
    top = None
    denom = None
    for r in range(TOP_K):
        mx = jnp.max(logits, axis=-1, keepdims=True)
        first = jnp.min(jnp.where(logits == mx, lane_f, float(LANES)), axis=-1, keepdims=True)
        hit = lane_f == first
        logits = jnp.where(hit, -jnp.inf, logits)
        chosen = jnp.where(hit, 1.0, chosen)
        if r == 0:
            top = mx
        e = jnp.exp(mx - top)
        denom = e if r == 0 else denom + e
        ids = jnp.where(lane == r, first, ids)
        wts = jnp.where(lane == r, e, wts)
    te_ref[...] = ids.astype(jnp.int32)
    tw_ref[...] = wts / denom
    cnt_ref[...] += _dot(jnp.ones((cnt_ref.shape[0], chosen.shape[0]), BF16), chosen.astype(BF16))


def _outproj_router(o_sb, o_sw, o_mb, x, mods, g_group, w_out_b, g_ffn, w_router_b, b_router_p, seq):
    n, d = x.shape
    tm = min(TOKEN_TILE, seq)
    per_batch = seq // tm
    row = lambda i: (i, 0)
    const = lambda i: (0, 0)
    return pl.pallas_call(
        _outproj_kernel,
        grid=(n // tm,),
        in_specs=[
            pl.BlockSpec((tm, SB_W), row),
            pl.BlockSpec((tm, SWA_W), row),
            pl.BlockSpec((tm, MOBA_W), row),
            pl.BlockSpec((tm, d), row),
            pl.BlockSpec((1, 6, d), lambda i: (i // per_batch, 0, 0)),
            pl.BlockSpec((1, d), const),
            pl.BlockSpec((d, d), const),
            pl.BlockSpec((1, d), const),
            pl.BlockSpec((d, LANES), const),
            pl.BlockSpec((1, LANES), const),
        ],
        out_specs=[
            pl.BlockSpec((tm, d), row),
            pl.BlockSpec((tm * SUBLANES, LANES), row),
            pl.BlockSpec((tm, LANES), row),
            pl.BlockSpec((tm, LANES), row),
            pl.BlockSpec((8, LANES), const),
        ],
        out_shape=[
            jax.ShapeDtypeStruct((n, d), F32),
            jax.ShapeDtypeStruct((n * SUBLANES, LANES), F32),
            jax.ShapeDtypeStruct((n, LANES), jnp.int32),
            jax.ShapeDtypeStruct((n, LANES), F32),
            jax.ShapeDtypeStruct((8, LANES), F32),
        ],
        compiler_params=pltpu.CompilerParams(dimension_semantics=("arbitrary",), vmem_limit_bytes=VMEM_LIMIT),
        name="outproj_router",
    )(o_sb, o_sw, o_mb, x, mods, g_group.reshape(1, d), w_out_b, g_ffn.reshape(1, d), w_router_b, b_router_p)


def _moe_kernel(be_ref, nu_ref, xs_ref, wgu_ref, bgu_ref, wd_ref, bd_ref, o_ref, wgu_b, wd_b):
    i = pl.program_id(0)
    d_ff = wd_b.shape[0]

    @pl.when(i < nu_ref[0])
    def _():
        changed = jnp.logical_or(i == 0, be_ref[i] != be_ref[jnp.maximum(i - 1, 0)])

        @pl.when(changed)
        def _():
            wgu_b[...] = wgu_ref[0, 0].astype(BF16)
            wd_b[...] = wd_ref[0, 0].astype(BF16)

        x = _from_row_tiles(xs_ref, MOE_ROWS).astype(BF16)
        gu = _dot(x, wgu_b[...]) + bgu_ref[0, 0]
        gate = jnp.minimum(gu[:, :d_ff], SWIGLU_LIMIT)
        up = jnp.clip(gu[:, d_ff:], -SWIGLU_LIMIT, SWIGLU_LIMIT)
        act = (up + 1.0) * gate * (1.0 / (1.0 + jnp.exp(-SWIGLU_ALPHA * gate)))
        _to_row_tiles(o_ref, _dot(act.astype(BF16), wd_b[...]) + bd_ref[0, 0])

    @pl.when(i >= nu_ref[0])
    def _():
        o_ref[...] = jnp.zeros_like(o_ref)


def _moe_experts(xs, block_e, n_used, layer, w_gate_up, b_gate_up, w_down, b_down):
    depth, n_exp, d, two_f = w_gate_up.shape
    n_rows = xs.shape[0] // SUBLANES
    d_ff = two_f // 2
    n_blocks = n_rows // MOE_ROWS
    blk = lambda i, be, nu: (jnp.minimum(i, nu[0] - 1), 0)
    grid_spec = pltpu.PrefetchScalarGridSpec(
        num_scalar_prefetch=2,
        grid=(n_blocks,),
        in_specs=[
            pl.BlockSpec((MOE_ROWS * SUBLANES, LANES), blk),
            pl.BlockSpec((1, 1, d, two_f), lambda i, be, nu: (layer, be[i], 0, 0)),
            pl.BlockSpec((1, 1, 1, two_f), lambda i, be, nu: (layer, be[i], 0, 0)),
            pl.BlockSpec((1, 1, d_ff, d), lambda i, be, nu: (layer, be[i], 0, 0)),
            pl.BlockSpec((1, 1, 1, d), lambda i, be, nu: (layer, be[i], 0, 0)),
        ],
        out_specs=pl.BlockSpec((MOE_ROWS * SUBLANES, LANES), lambda i, be, nu: (i, 0)),
        scratch_shapes=[
            pltpu.VMEM((d, two_f), BF16),
            pltpu.VMEM((d_ff, d), BF16),
        ],
    )
    return pl.pallas_call(
        _moe_kernel,
        grid_spec=grid_spec,
        out_shape=jax.ShapeDtypeStruct((n_rows * SUBLANES, LANES), F32),
        compiler_params=pltpu.CompilerParams(dimension_semantics=("arbitrary",), vmem_limit_bytes=VMEM_LIMIT),
        name="moe_experts",
    )(block_e, n_used, xs, w_gate_up, b_gate_up.reshape(depth, n_exp, 1, two_f), w_down,
      b_down.reshape(depth, n_exp, 1, d))


def _combine_kernel(pos_ref, pos_next_ref, ys_hbm, x_ref, tw_ref, m_ref, gfin_ref, o_ref, buf, pos_smem,
                    sem_rows, sem_pos, *, final):
    i = pl.program_id(0)
    n_steps = pl.num_programs(0)
    tm = x_ref.shape[0]
    slot = i % 2

    def row_copy(s, t, k, row):
        src = pl.multiple_of(row * SUBLANES, SUBLANES)
        dst = pl.multiple_of(t * SUBLANES, SUBLANES)
        return pltpu.make_async_copy(ys_hbm.at[pl.ds(src, SUBLANES)], buf.at[s, k, pl.ds(dst, SUBLANES)],
                                     sem_rows.at[s])

    def for_row_groups(s, act):
        def body(g, carry):
            t0 = g * ROW_GROUP
            rows = [[pos_smem[s, t0 + j, k] for k in range(TOP_K)] for j in range(ROW_GROUP)]
            for j in range(ROW_GROUP):
                for k in range(TOP_K):
                    act(row_copy(s, t0 + j, k, rows[j][k]), k % DMA_PRIORITIES)
            return carry

        lax.fori_loop(0, tm // ROW_GROUP, body, 0)

    def start_gather(src_ref, s):
        to_smem = pltpu.make_async_copy(src_ref, pos_smem.at[s], sem_pos)
        to_smem.start()
        to_smem.wait()
        for_row_groups(s, lambda cp, prio: cp.start(priority=prio))

    @pl.when(i == 0)
    def _():
        start_gather(pos_ref, 0)

    @pl.when(i + 1 < n_steps)
    def _():
        start_gather(pos_next_ref, 1 - slot)

    for_row_groups(slot, lambda cp, prio: cp.wait())

    tw = tw_ref[...]
    cols = []
    for s in range(SUBLANES):
        acc = tw[:, 0:1] * buf.at[slot, 0][pl.ds(s, tm, stride=SUBLANES), :]
        for k in range(1, TOP_K):
            acc = acc + tw[:, k:k + 1] * buf.at[slot, k][pl.ds(s, tm, stride=SUBLANES), :]
        cols.append(acc)
    x = x_ref[...] + m_ref[0][5:6] * jnp.concatenate(cols, axis=1)
    if final:
        x = _rms(x, gfin_ref[...])
    o_ref[...] = x


def _moe_combine(ys, pos, x, tw, mods, g_final, seq, final):
    n, d = x.shape
    tm = min(COMBINE_TILE, seq)
    per_batch = seq // tm
    n_tiles = n // tm
    return pl.pallas_call(
        functools.partial(_combine_kernel, final=final),
        grid=(n_tiles,),
        in_specs=[
            pl.BlockSpec((tm, LANES), lambda i: (i, 0)),
            pl.BlockSpec((tm, LANES), lambda i: (jnp.minimum(i + 1, n_tiles - 1), 0)),
            pl.BlockSpec(memory_space=pl.ANY),
            pl.BlockSpec((tm, d), lambda i: (i, 0)),
            pl.BlockSpec((tm, LANES), lambda i: (i, 0)),
            pl.BlockSpec((1, 6, d), lambda i: (i // per_batch, 0, 0)),
            pl.BlockSpec((1, d), lambda i: (0, 0)),
        ],
        out_specs=pl.BlockSpec((tm, d), lambda i: (i, 0)),
        out_shape=jax.ShapeDtypeStruct((n, d), F32),
        scratch_shapes=[pltpu.VMEM((2, TOP_K, tm * SUBLANES, LANES), F32), pltpu.SMEM((2, tm, LANES), jnp.int32),
                        pltpu.SemaphoreType.DMA((2,)), pltpu.SemaphoreType.DMA],
        compiler_params=pltpu.CompilerParams(dimension_semantics=("arbitrary",), vmem_limit_bytes=VMEM_LIMIT),
        name="moe_combine",
    )(pos, pos, ys, x, tw, mods, g_final.reshape(1, d))


def _row_plan(counts, n_tok):
    cnt = counts[0, :N_EXPERTS].astype(jnp.int32)
    padded = (cnt + MOE_ROWS - 1) // MOE_ROWS * MOE_ROWS
    pad_end = jnp.cumsum(padded)
    pad_start = pad_end - padded
    n_blocks = -(-n_tok * TOP_K // MOE_ROWS) + N_EXPERTS
    n_used = (pad_end[-1] // MOE_ROWS).astype(jnp.int32)
    blk_first = jnp.minimum(jnp.arange(n_blocks, dtype=jnp.int32), n_used - 1) * MOE_ROWS
    block_e = jnp.sum(blk_first[:, None] >= pad_end[None, :], axis=1).astype(jnp.int32)
    pad_end0 = jnp.concatenate([jnp.zeros((1,), jnp.int32), pad_end.astype(jnp.int32)])
    pad_start_row = jnp.zeros((1, LANES), F32).at[0, :N_EXPERTS].set(pad_start.astype(F32))
    return pad_end0, pad_start_row, block_e, n_used.reshape(1), n_blocks * MOE_ROWS


def _dispatch_kernel(pend_ref, te_ref, h_hbm, pstart_ref, pos_ref, xs_hbm, carry_ref, zero_ref, hbuf, pos_smem,
                     sem_rows, sem_load, sem_misc):
    i = pl.program_id(0)
    n_steps = pl.num_programs(0)
    tm = te_ref.shape[0]
    blk = MOE_ROWS * SUBLANES
    tile = tm * SUBLANES
    lane = lax.broadcasted_iota(jnp.int32, (tm, LANES), 1)

    def tile_load(step):
        src = pl.multiple_of(step * tile, tile)
        return pltpu.make_async_copy(h_hbm.at[pl.ds(src, tile)], hbuf.at[step % 3], sem_load.at[step % 3])

    @pl.when(i == 0)
    def _():
        tile_load(0).start()

    @pl.when(i + 1 < n_steps)
    def _():
        tile_load(i + 1).start()

    @pl.when(i == 0)
    def _():
        carry_ref[...] = jnp.zeros_like(carry_ref)
        zero_ref[...] = jnp.zeros_like(zero_ref)

        def tail_copy(e):
            end = pl.multiple_of(pend_ref[e + 1] * SUBLANES, blk)
            return pltpu.make_async_copy(zero_ref, xs_hbm.at[pl.ds(end - blk, blk)], sem_misc)

        for e in range(N_EXPERTS):
            @pl.when(pend_ref[e + 1] > pend_ref[e])
            def _():
                tail_copy(e).start()
        for e in range(N_EXPERTS):
            @pl.when(pend_ref[e + 1] > pend_ref[e])
            def _():
                tail_copy(e).wait()

        def spare_copy(b):
            return pltpu.make_async_copy(zero_ref, xs_hbm.at[pl.ds(pl.multiple_of(b * blk, blk), blk)], sem_misc)

        first_spare = pend_ref[N_EXPERTS] // MOE_ROWS
        n_blocks = xs_hbm.shape[0] // blk

        def start_spare(b, carry):
            spare_copy(b).start()
            return carry

        def wait_spare(b, carry):
            spare_copy(b).wait()
            return carry

        lax.fori_loop(first_spare, n_blocks, start_spare, 0)
        lax.fori_loop(first_spare, n_blocks, wait_spare, 0)

    te = te_ref[...]
    hits = [lane == te[:, k:k + 1] for k in range(TOP_K)]
    cnt = jnp.zeros((tm, LANES), F32)
    for k in range(TOP_K):
        cnt = cnt + jnp.where(hits[k], 1.0, 0.0)
    cnt_b = cnt.astype(BF16)
    r = lax.broadcasted_iota(jnp.int32, (tm, tm), 0)
    c = lax.broadcasted_iota(jnp.int32, (tm, tm), 1)
    before = jnp.where(c < r, 1.0, 0.0).astype(BF16)
    base = _dot(before, cnt_b) + (carry_ref[0:1, :] + pstart_ref[...])
    pos = jnp.zeros((tm, LANES), jnp.int32)
    for k in range(TOP_K):
        pk = jnp.sum(jnp.where(hits[k], base, 0.0), axis=-1, keepdims=True).astype(jnp.int32)
        pos = jnp.where(lane == k, pk, pos)
    pos_ref[...] = pos
    carry_ref[...] = carry_ref[...] + _dot(jnp.ones((carry_ref.shape[0], tm), BF16), cnt_b)

    slot = i % 2
    to_smem = pltpu.make_async_copy(pos_ref, pos_smem.at[slot], sem_misc)
    to_smem.start()
    to_smem.wait()

    def row_copy(step, s, t, row):
        src = pl.multiple_of(t * SUBLANES, SUBLANES)
        dst = pl.multiple_of(row * SUBLANES, SUBLANES)
        return pltpu.make_async_copy(hbuf.at[step % 3, pl.ds(src, SUBLANES)], xs_hbm.at[pl.ds(dst, SUBLANES)],
                                     sem_rows.at[s])

    def for_row_groups(step, s, act):
        def body(g, carry):
            t0 = g * ROW_GROUP
            rows = [[pos_smem[s, t0 + j, k] for k in range(TOP_K)] for j in range(ROW_GROUP)]
            for j in range(ROW_GROUP):
                for k in range(TOP_K):
                    act(row_copy(step, s, t0 + j, rows[j][k]), k % DMA_PRIORITIES)
            return carry

        lax.fori_loop(0, tm // ROW_GROUP, body, 0)

    tile_load(i).wait()
    for_row_groups(i, slot, lambda cp, prio: cp.start(priority=prio))

    def drain(step, s):
        for_row_groups(step, s, lambda cp, prio: cp.wait())

    @pl.when(i >= 1)
    def _():
        drain(i - 1, 1 - slot)

    @pl.when(i == n_steps - 1)
    def _():
        drain(i, slot)


def _moe_dispatch(top_e, h, pad_end0, pad_start_row, n_rows, seq):
    n = top_e.shape[0]
    tm = min(DISPATCH_TILE, seq)
    grid_spec = pltpu.PrefetchScalarGridSpec(
        num_scalar_prefetch=1,
        grid=(n // tm,),
        in_specs=[
            pl.BlockSpec((tm, LANES), lambda i, pe: (i, 0)),
            pl.BlockSpec(memory_space=pl.ANY),
            pl.BlockSpec((1, LANES), lambda i, pe: (0, 0)),
        ],
        out_specs=[
            pl.BlockSpec((tm, LANES), lambda i, pe: (i, 0)),
            pl.BlockSpec(memory_space=pl.ANY),
        ],
        scratch_shapes=[
            pltpu.VMEM((8, LANES), F32),
            pltpu.VMEM((MOE_ROWS * SUBLANES, LANES), F32),
            pltpu.VMEM((3, tm * SUBLANES, LANES), F32),
            pltpu.SMEM((2, tm, LANES), jnp.int32),
            pltpu.SemaphoreType.DMA((2,)),
            pltpu.SemaphoreType.DMA((3,)),
            pltpu.SemaphoreType.DMA,
        ],
    )
    return pl.pallas_call(
        _dispatch_kernel,
        grid_spec=grid_spec,
        out_shape=[jax.ShapeDtypeStruct((n, LANES), jnp.int32),
                   jax.ShapeDtypeStruct((n_rows * SUBLANES, LANES), F32)],
        compiler_params=pltpu.CompilerParams(dimension_semantics=("arbitrary",), vmem_limit_bytes=VMEM_LIMIT),
        name="moe_dispatch",
    )(pad_end0, top_e, h, pad_start_row)


def _t5_bucket(dist):
    n = jnp.maximum(dist, 0)
    max_exact = NUM_BUCKETS // 2
    nf = jnp.maximum(n, 1).astype(F32)
    large = max_exact + (jnp.log(nf / max_exact) / math.log(MAX_DISTANCE / max_exact)
                         * (NUM_BUCKETS - max_exact)).astype(jnp.int32)
    large = jnp.minimum(large, NUM_BUCKETS - 1)
    return jnp.where(n < max_exact, n, large)


def _bias_lookup(table, dist):
    onehot = (_t5_bucket(dist)[..., None] == jnp.arange(NUM_BUCKETS)).astype(F32)
    out = jnp.einsum("...k,kh->h...", onehot, table.astype(F32), precision=lax.Precision.HIGHEST)
    return out.astype(F32)


def _bias_tables(rel_bias):
    r = jnp.arange(Q_BLOCK)[:, None]
    j2 = jnp.arange(2 * Q_BLOCK)[None, :]
    swa = _bias_lookup(rel_bias[:, :SWA_HEADS], Q_BLOCK + r - j2)
    tab = rel_bias[:, SWA_HEADS:]
    rb = jnp.arange(MOBA_BLOCK)[:, None]
    jb = jnp.arange(2 * MOBA_BLOCK)[None, :]
    top = _bias_lookup(tab, rb + MOBA_BLOCK - jb)
    far = tab[NUM_BUCKETS - 1]
    return swa, top, far.astype(F32)


def kernel(x, c, w_in, w_out, g_norm_mix, g_norm_ffn, g_group, w_mod, b_mod, swa_sinks, rel_bias,
           w_router, b_router, w_gate_up, b_gate_up, w_down, b_down, g_final):
    batch, seq, d = x.shape
    depth = w_in.shape[0]
    n = batch * seq
    assert seq % MOBA_BLOCK == 0 and seq // MOBA_BLOCK <= LANES and d == SUBLANES * LANES

    mods = _modulation(c, w_mod, b_mod)
    bias_swa, bias_top, bias_far = _bias_tables(rel_bias)
    w_in_b = w_in.astype(BF16)
    w_out_b = w_out.astype(BF16)
    w_router_b = jnp.zeros((depth, d, LANES), BF16).at[:, :, :N_EXPERTS].set(w_router.astype(BF16))
    b_router_p = jnp.full((depth, 1, LANES), NEG, F32).at[:, 0, :N_EXPERTS].set(b_router)

    xf = x.reshape(n, d)
    for l in range(depth):
        proj = _inproj(xf, g_norm_mix[l], mods[l], w_in_b[l], seq)
        o_sb = _sb_attention(proj, batch, seq)
        o_sw = _swa_attention(proj, swa_sinks[l], bias_swa, batch, seq)
        o_mb = _moba_attention(proj, bias_far, bias_top, batch, seq)
        xf, h, top_e, top_w, counts = _outproj_router(o_sb, o_sw, o_mb, xf, mods[l], g_group[l], w_out_b[l],
                                                      g_norm_ffn[l], w_router_b[l], b_router_p[l], seq)
        pad_end0, pad_start_row, block_e, n_used, n_rows = _row_plan(counts, n)
        pos, xs = _moe_dispatch(top_e, h, pad_end0, pad_start_row, n_rows, seq)
        ys = _moe_experts(xs, block_e, n_used, l, w_gate_up, b_gate_up, w_down, b_down)
        xf = _moe_combine(ys, pos, xf, top_w, mods[l], g_final, seq, final=(l == depth - 1))
    return xf.reshape(batch, seq, d)
```

```python
import functools
import math

import jax
import jax.numpy as jnp
from jax import lax
from jax.experimental import pallas as pl
from jax.experimental.pallas import tpu as pltpu

F32 = jnp.float32
BF16 = jnp.bfloat16

HEAD_DIM = 64
SB_HEADS = 4
SWA_HEADS = 8
SWA_KV_HEADS = 2
MOBA_HEADS = 4
SB_W = SB_HEADS * HEAD_DIM
SWA_W = SWA_HEADS * HEAD_DIM
SWA_KV_W = SWA_KV_HEADS * HEAD_DIM
MOBA_W = MOBA_HEADS * HEAD_DIM
Q_BLOCK = 128
WINDOW = 128
MOBA_BLOCK = 256
MOBA_TOPK = 3
NUM_BUCKETS = 32
MAX_DISTANCE = 128
N_EXPERTS = 32
TOP_K = 4
SWIGLU_LIMIT = 7.0
SWIGLU_ALPHA = 1.702
MOE_BLOCK = 256
EPS = 1e-6
NEG = -1e30
ATTN_SCALE = HEAD_DIM ** -0.5

LANES = 128
SUBLANES = 8
COL_SB_Q, COL_SB_K, COL_SB_V = 0, 2, 4
COL_SW_Q, COL_SW_K, COL_SW_V = 6, 10, 11
COL_MB_Q, COL_MB_K, COL_MB_V = 12, 14, 16
IN_W = 18 * LANES

SB_SPAN = 512
SB_QUERIES = 256
MOBA_GROUP = 4
TOKEN_TILE = 512
MOE_ROWS = 512
DISPATCH_TILE = 256
COMBINE_TILE = 256
ROW_GROUP = 8
DMA_PRIORITIES = 2
VMEM_LIMIT = 56 * 1024 * 1024


def _rms(x, g):
    return x * lax.rsqrt(jnp.mean(x * x, axis=-1, keepdims=True) + EPS) * g


def _dot_t(a, b):
    return lax.dot_general(a, b, (((1,), (1,)), ((), ())), preferred_element_type=F32)


def _dot(a, b):
    return jnp.dot(a, b, preferred_element_type=F32)


def _to_row_tiles(ref, x):
    rows = x.shape[0]
    for s in range(SUBLANES):
        ref[pl.ds(s, rows, stride=SUBLANES), :] = x[:, s * LANES:(s + 1) * LANES]


def _from_row_tiles(ref, rows):
    return jnp.concatenate([ref[pl.ds(s, rows, stride=SUBLANES), :] for s in range(SUBLANES)], axis=1)


def _mod_kernel(c_ref, w_ref, b_ref, o_ref):
    c = c_ref[...]
    ca = c * (1.0 / (1.0 + jnp.exp(-c)))
    o_ref[0] = _dot(ca, w_ref[0]) + b_ref[0]


def _modulation(c, w_mod, b_mod):
    depth, d, six_d = w_mod.shape
    b = c.shape[0]
    rows = 8
    c8 = jnp.zeros((rows, d), F32).at[:b].set(c)
    tn = six_d // 6
    out = pl.pallas_call(
        _mod_kernel,
        grid=(depth, six_d // tn),
        in_specs=[
            pl.BlockSpec((rows, d), lambda l, j: (0, 0)),
            pl.BlockSpec((1, d, tn), lambda l, j: (l, 0, j)),
            pl.BlockSpec((1, 1, tn), lambda l, j: (l, 0, j)),
        ],
        out_specs=pl.BlockSpec((1, rows, tn), lambda l, j: (l, 0, j)),
        out_shape=jax.ShapeDtypeStruct((depth, rows, six_d), F32),
        name="adaln_mod",
    )(c8, w_mod, b_mod.reshape(depth, 1, six_d))
    return out[:, :b].reshape(depth, b, 6, d)


def _inproj_kernel(x_ref, g_ref, m_ref, w_ref, o_ref):
    m = m_ref[0]
    h = _rms(x_ref[...], g_ref[...]) * (1.0 + m[1:2]) + m[0:1]
    hb = h.astype(BF16)
    step = 2 * LANES
    for j in range(IN_W // step):
        o_ref[:, j * step:(j + 1) * step] = _dot(hb, w_ref[:, j * step:(j + 1) * step]).astype(BF16)


def _inproj(x, g, mods, w_in_b, seq):
    n, d = x.shape
    tm = min(TOKEN_TILE, seq)
    per_batch = seq // tm
    return pl.pallas_call(
        _inproj_kernel,
        grid=(n // tm,),
        in_specs=[
            pl.BlockSpec((tm, d), lambda i: (i, 0)),
            pl.BlockSpec((1, d), lambda i: (0, 0)),
            pl.BlockSpec((1, 6, d), lambda i: (i // per_batch, 0, 0)),
            pl.BlockSpec((d, IN_W), lambda i: (0, 0)),
        ],
        out_specs=pl.BlockSpec((tm, IN_W), lambda i: (i, 0)),
        out_shape=jax.ShapeDtypeStruct((n, IN_W), BF16),
        compiler_params=pltpu.CompilerParams(dimension_semantics=("arbitrary",), vmem_limit_bytes=VMEM_LIMIT),
        name="norm_inproj",
    )(x, g.reshape(1, d), mods, w_in_b)


def _sb_kernel(q_ref, k_ref, v_ref, o_ref, *, span):
    i = pl.program_id(2)
    tq = q_ref.shape[0]
    n_chunks = span // LANES
    lane = lax.broadcasted_iota(jnp.int32, (tq, LANES), 1)
    tri_r = lax.broadcasted_iota(jnp.int32, (LANES, LANES), 0)
    tri_c = lax.broadcasted_iota(jnp.int32, (LANES, LANES), 1)
    tri = jnp.where(tri_r > tri_c, 1.0, 0.0).astype(BF16)
    q = q_ref[...] * ATTN_SCALE
    qms = [jnp.where((lane >= HEAD_DIM * hh) & (lane < HEAD_DIM * (hh + 1)), q, jnp.zeros_like(q))
           for hh in range(2)]
    q_start = i * tq
    key_off = lax.broadcasted_iota(jnp.int32, (tq, span), 1) - lax.broadcasted_iota(jnp.int32, (tq, span), 0)

    def do_span(sidx, state, masked):
        start = pl.multiple_of(sidx * span, span)
        ks = k_ref[pl.ds(start, span), :]
        vs = v_ref[pl.ds(start, span), :]
        new_state = []
        for hh in range(2):
            run, acc = state[hh]
            z = _dot_t(qms[hh], ks).astype(BF16)
            l1p = jnp.log(1.0 + jnp.exp(-jnp.abs(z)))
            mn = jnp.minimum(z, 0.0)
            log_beta = mn - l1p
            log_1m = (mn - z) - l1p
            parts = [None] * n_chunks
            for c in reversed(range(n_chunks)):
                cols = slice(c * LANES, (c + 1) * LANES)
                lc = log_1m[:, cols]
                if masked:
                    strict = (key_off[:, cols] + (start - q_start)) < 0
                    lc = jnp.where(strict, lc, jnp.zeros_like(lc))
                later = _dot(lc, tri) + run
                a = jnp.exp(log_beta[:, cols].astype(F32) + later)
                if masked:
                    a = jnp.where(strict, a, 0.0)
                parts[c] = a.astype(BF16)
                run = run + jnp.sum(lc.astype(F32), axis=-1, keepdims=True)
            acc = acc + _dot(jnp.concatenate(parts, axis=1), vs)
            new_state.append((run, acc))
        return tuple(new_state)

    init = ((jnp.zeros((tq, 1), F32), jnp.zeros((tq, LANES), F32)),) * 2
    top = q_start // span
    state = do_span(top, init, True)
    state = lax.fori_loop(0, top, lambda jj, st: do_span(top - 1 - jj, st, False), state)
    o_ref[...] = jnp.where(lane < HEAD_DIM, state[0][1], state[1][1])


def _sb_attention(proj, batch, seq):
    n = proj.shape[0]
    span = min(SB_SPAN, seq)
    tq = min(SB_QUERIES, span)
    assert span % tq == 0 and seq % span == 0
    nq = seq // tq
    pairs = SB_HEADS // 2
    return pl.pallas_call(
        functools.partial(_sb_kernel, span=span),
        grid=(batch, pairs, nq),
        in_specs=[
            pl.BlockSpec((tq, LANES), lambda b, p, i: (b * nq + i, COL_SB_Q + p)),
            pl.BlockSpec((seq, LANES), lambda b, p, i: (b, COL_SB_K + p)),
            pl.BlockSpec((seq, LANES), lambda b, p, i: (b, COL_SB_V + p)),
        ],
        out_specs=pl.BlockSpec((tq, LANES), lambda b, p, i: (b * nq + i, p)),
        out_shape=jax.ShapeDtypeStruct((n, SB_W), F32),
        compiler_params=pltpu.CompilerParams(
            dimension_semantics=("arbitrary", "arbitrary", "arbitrary"), vmem_limit_bytes=VMEM_LIMIT),
        name="sb_attention",
    )(proj, proj, proj)


def _swa_kernel(sink_ref, qa_ref, qb_ref, kp_ref, kc_ref, vp_ref, vc_ref, bias_ref, o_ref):
    i = pl.program_id(1)
    kk = jnp.concatenate([kp_ref[...], kc_ref[...]], axis=0)
    vv = jnp.concatenate([vp_ref[...], vc_ref[...]], axis=0)
    group = SWA_HEADS // SWA_KV_HEADS
    rows = group * Q_BLOCK
    r = lax.broadcasted_iota(jnp.int32, (rows, 2 * Q_BLOCK), 0)
    j = lax.broadcasted_iota(jnp.int32, (rows, 2 * Q_BLOCK), 1)
    dist = Q_BLOCK + r % Q_BLOCK - j
    valid = (dist >= 0) & (dist < WINDOW) & ((j >= Q_BLOCK) | (i > 0))
    head_of_row = lax.broadcasted_iota(jnp.int32, (rows, 1), 0) // Q_BLOCK
    for g, q_ref in enumerate((qa_ref, qb_ref)):
        q = q_ref[...] * ATTN_SCALE
        qs = jnp.concatenate([q[:, HEAD_DIM * c:HEAD_DIM * (c + 1)] for c in range(group)], axis=0)
        bias = bias_ref[group * g:group * (g + 1)].reshape(rows, 2 * Q_BLOCK)
        s = _dot_t(qs, kk[:, HEAD_DIM * g:HEAD_DIM * (g + 1)]) + bias
        s = jnp.where(valid, s, NEG)
        sink = jnp.zeros((rows, 1), F32)
        for c in range(group):
            sink = jnp.where(head_of_row == c, sink_ref[group * g + c], sink)
        m = jnp.maximum(jnp.max(s, axis=-1, keepdims=True), sink)
        p = jnp.exp(s - m)
        l = jnp.sum(p, axis=-1, keepdims=True) + jnp.exp(sink - m)
        o = _dot(p.astype(BF16), vv[:, HEAD_DIM * g:HEAD_DIM * (g + 1)]) / l
        for c in range(group):
            h = group * g + c
            o_ref[:, HEAD_DIM * h:HEAD_DIM * (h + 1)] = o[Q_BLOCK * c:Q_BLOCK * (c + 1)]


def _swa_attention(proj, sinks, bias, batch, seq):
    n = proj.shape[0]
    nq = seq // Q_BLOCK
    wide = 2 * LANES
    grid_spec = pltpu.PrefetchScalarGridSpec(
        num_scalar_prefetch=1,
        grid=(batch, nq),
        in_specs=[
            pl.BlockSpec((Q_BLOCK, wide), lambda b, i, s: (b * nq + i, COL_SW_Q // 2)),
            pl.BlockSpec((Q_BLOCK, wide), lambda b, i, s: (b * nq + i, COL_SW_Q // 2 + 1)),
            pl.BlockSpec((Q_BLOCK, LANES), lambda b, i, s: (b * nq + jnp.maximum(i - 1, 0), COL_SW_K)),
            pl.BlockSpec((Q_BLOCK, LANES), lambda b, i, s: (b * nq + i, COL_SW_K)),
            pl.BlockSpec((Q_BLOCK, LANES), lambda b, i, s: (b * nq + jnp.maximum(i - 1, 0), COL_SW_V)),
            pl.BlockSpec((Q_BLOCK, LANES), lambda b, i, s: (b * nq + i, COL_SW_V)),
            pl.BlockSpec((SWA_HEADS, Q_BLOCK, 2 * Q_BLOCK), lambda b, i, s: (0, 0, 0)),
        ],
        out_specs=pl.BlockSpec((Q_BLOCK, SWA_W), lambda b, i, s: (b * nq + i, 0)),
    )
    return pl.pallas_call(
        _swa_kernel,
        grid_spec=grid_spec,
        out_shape=jax.ShapeDtypeStruct((n, SWA_W), F32),
        compiler_params=pltpu.CompilerParams(dimension_semantics=("arbitrary", "arbitrary")),
        name="swa_attention",
    )(sinks, proj, proj, proj, proj, proj, proj, bias)


def _moba_kernel(far_ref, q_ref, k_ref, v_ref, btop_ref, o_ref, kmean_ref, rhs_ref, *, group):
    p = pl.program_id(1)
    cur = pl.program_id(2)
    seq = k_ref.shape[0]
    tq = q_ref.shape[0]
    nb = seq // MOBA_BLOCK
    top_w = 2 * MOBA_BLOCK
    far_w = group * MOBA_BLOCK
    lane = lax.broadcasted_iota(jnp.int32, (tq, LANES), 1)

    @pl.when(cur == 0)
    def _():
        blk = lax.broadcasted_iota(jnp.int32, (LANES, seq), 0)
        pos = lax.broadcasted_iota(jnp.int32, (LANES, seq), 1)
        lo = blk * MOBA_BLOCK
        avg = jnp.where((pos >= lo) & (pos < lo + MOBA_BLOCK), 1.0 / MOBA_BLOCK, 0.0).astype(BF16)
        kmean_ref[...] = _dot(avg, k_ref[...]).astype(BF16)
        key_blk = lax.broadcasted_iota(jnp.int32, (seq, LANES), 0) // MOBA_BLOCK
        blk_lane = lax.broadcasted_iota(jnp.int32, (seq, LANES), 1)
        rhs_ref[:, :LANES] = k_ref[...]
        rhs_ref[:, LANES:] = jnp.where(key_blk == blk_lane, 1.0, 0.0).astype(BF16)

    q = q_ref[...]
    kmean = kmean_ref[...]
    lane_f = lane.astype(F32)

    prev_start = pl.multiple_of(jnp.maximum(cur - 1, 0) * MOBA_BLOCK, MOBA_BLOCK)
    own_start = pl.multiple_of(cur * MOBA_BLOCK, MOBA_BLOCK)
    rhs_top = jnp.concatenate([rhs_ref[pl.ds(prev_start, MOBA_BLOCK), :], rhs_ref[pl.ds(own_start, MOBA_BLOCK), :]],
                              axis=0)
    v_top = jnp.concatenate([v_ref[pl.ds(prev_start, MOBA_BLOCK), :], v_ref[pl.ds(own_start, MOBA_BLOCK), :]], axis=0)
    jt = lax.broadcasted_iota(jnp.int32, (tq, top_w), 1)
    rt = lax.broadcasted_iota(jnp.int32, (tq, top_w), 0)
    top_ok = jnp.logical_and(jt < MOBA_BLOCK, cur >= 1) | ((jt >= MOBA_BLOCK) & (rt >= jt - MOBA_BLOCK))

    def update(state, s, vb):
        m, l, acc = state
        m_new = jnp.maximum(m, jnp.max(s, axis=-1, keepdims=True))
        alpha = jnp.exp(m - m_new)
        pexp = jnp.exp(s - m_new)
        l = alpha * l + jnp.sum(pexp, axis=-1, keepdims=True)
        acc = alpha * acc + _dot(pexp.astype(BF16), vb)
        return m_new, l, acc

    lhs_far, states = [], []
    for hh in range(2):
        hmask = (lane >= HEAD_DIM * hh) & (lane < HEAD_DIM * (hh + 1))
        qg = jnp.where(hmask, q, jnp.zeros_like(q))
        qm = qg * ATTN_SCALE
        gate = jnp.where(lane < cur, _dot_t(qg, kmean), NEG)
        sel = jnp.zeros((tq, LANES), jnp.bool_)
        for _ in range(min(MOBA_TOPK, nb)):
            mx = jnp.max(gate, axis=-1, keepdims=True)
            first = jnp.min(jnp.where(gate == mx, lane_f, float(LANES)), axis=-1, keepdims=True)
            hit = lane_f == first
            sel = sel | hit
            gate = jnp.where(hit, -jnp.inf, gate)
        open_top = (sel & (lane == cur - 1)) | (lane == cur)
        open_far = sel & (lane < cur - 1)
        lhs_top = jnp.concatenate([qm, jnp.where(open_top, 0.0, NEG).astype(BF16)], axis=1)
        s = _dot_t(lhs_top, rhs_top) + btop_ref[hh]
        s = jnp.where(top_ok, s, NEG)
        m = jnp.max(s, axis=-1, keepdims=True)
        pexp = jnp.exp(s - m)
        l = jnp.sum(pexp, axis=-1, keepdims=True)
        acc = _dot(pexp.astype(BF16), v_top)
        lhs_far.append(jnp.concatenate([qm, jnp.where(open_far, 0.0, NEG).astype(BF16)], axis=1))
        states.append((m, l, acc))

    def far_group(g, states):
        start = pl.multiple_of(g * far_w, far_w)
        rhs = rhs_ref[pl.ds(start, far_w), :]
        vg = v_ref[pl.ds(start, far_w), :]
        return tuple(update(states[hh], _dot_t(lhs_far[hh], rhs) + far_ref[2 * p + hh], vg) for hh in range(2))

    n_far = jnp.maximum(cur - 1, 0)
    states = lax.fori_loop(0, (n_far + group - 1) // group, far_group, tuple(states))
    outs = [acc / l for (_, l, acc) in states]
    o_ref[...] = jnp.where(lane < HEAD_DIM, outs[0], outs[1])


def _moba_attention(proj, far_bias, bias_top, batch, seq):
    n = proj.shape[0]
    nq = seq // MOBA_BLOCK
    pairs = MOBA_HEADS // 2
    nb = seq // MOBA_BLOCK
    group = min(MOBA_GROUP, nb)
    assert nb % group == 0
    grid_spec = pltpu.PrefetchScalarGridSpec(
        num_scalar_prefetch=1,
        grid=(batch, pairs, nq),
        in_specs=[
            pl.BlockSpec((MOBA_BLOCK, LANES), lambda b, p, i, f: (b * nq + i, COL_MB_Q + p)),
            pl.BlockSpec((seq, LANES), lambda b, p, i, f: (b, COL_MB_K + p)),
            pl.BlockSpec((seq, LANES), lambda b, p, i, f: (b, COL_MB_V + p)),
            pl.BlockSpec((2, MOBA_BLOCK, 2 * MOBA_BLOCK), lambda b, p, i, f: (p, 0, 0)),
        ],
        out_specs=pl.BlockSpec((MOBA_BLOCK, LANES), lambda b, p, i, f: (b * nq + i, p)),
        scratch_shapes=[pltpu.VMEM((LANES, LANES), BF16), pltpu.VMEM((seq, 2 * LANES), BF16)],
    )
    return pl.pallas_call(
        functools.partial(_moba_kernel, group=group),
        grid_spec=grid_spec,
        out_shape=jax.ShapeDtypeStruct((n, MOBA_W), F32),
        compiler_params=pltpu.CompilerParams(
            dimension_semantics=("arbitrary", "arbitrary", "arbitrary"), vmem_limit_bytes=VMEM_LIMIT),
        name="moba_attention",
    )(far_bias, proj, proj, proj, bias_top)


def _outproj_kernel(osb_ref, osw_ref, omb_ref, x_ref, m_ref, gg_ref, wo_ref, gf_ref, wr_ref, br_ref,
                    xo_ref, h_ref, te_ref, tw_ref, cnt_ref):
    m = m_ref[0]

    @pl.when(pl.program_id(0) == 0)
    def _():
        cnt_ref[...] = jnp.zeros_like(cnt_ref)

    c1, c2 = SB_W, SB_W + SWA_W
    y = _dot(_rms(osb_ref[...], gg_ref[:, :c1]).astype(BF16), wo_ref[:c1, :])
    y = y + _dot(_rms(osw_ref[...], gg_ref[:, c1:c2]).astype(BF16), wo_ref[c1:c2, :])
    y = y + _dot(_rms(omb_ref[...], gg_ref[:, c2:]).astype(BF16), wo_ref[c2:, :])
    x = x_ref[...] + m[2:3] * y
    xo_ref[...] = x
    h = _rms(x, gf_ref[...]) * (1.0 + m[4:5]) + m[3:4]
    _to_row_tiles(h_ref, h)
    logits = _dot(h.astype(BF16), wr_ref[...]) + br_ref[...]
    lane = lax.broadcasted_iota(jnp.int32, logits.shape, 1)
    lane_f = lane.astype(F32)
    ids = jnp.zeros(logits.shape, F32)
    wts = jnp.zeros(logits.shape, F32)
    chosen = jnp.zeros(logits.shape, F32)
    top = None
    denom = None
    for r in range(TOP_K):
        mx = jnp.max(logits, axis=-1, keepdims=True)
        first = jnp.min(jnp.where(logits == mx, lane_f, float(LANES)), axis=-1, keepdims=True)
        hit = lane_f == first
        logits = jnp.where(hit, -jnp.inf, logits)
        chosen = jnp.where(hit, 1.0, chosen)
        if r == 0:
            top = mx
        e = jnp.exp(mx - top)
        denom = e if r == 0 else denom + e
        ids = jnp.where(lane == r, first, ids)
        wts = jnp.where(lane == r, e, wts)
    te_ref[...] = ids.astype(jnp.int32)
    tw_ref[...] = wts / denom
    cnt_ref[...] += _dot(jnp.ones((cnt_ref.shape[0], chosen.shape[0]), BF16), chosen.astype(BF16))


def _outproj_router(o_sb, o_sw, o_mb, x, mods, g_group, w_out_b, g_ffn, w_router_b, b_router_p, seq):
    n, d = x.shape
    tm = min(TOKEN_TILE, seq)
    per_batch = seq // tm
    row = lambda i: (i, 0)
    const = lambda i: (0, 0)
    return pl.pallas_call(
        _outproj_kernel,
        grid=(n // tm,),
        in_specs=[
            pl.BlockSpec((tm, SB_W), row),
            pl.BlockSpec((tm, SWA_W), row),
            pl.BlockSpec((tm, MOBA_W), row),
            pl.BlockSpec((tm, d), row),
            pl.BlockSpec((1, 6, d), lambda i: (i // per_batch, 0, 0)),
            pl.BlockSpec((1, d), const),
            pl.BlockSpec((d, d), const),
            pl.BlockSpec((1, d), const),
            pl.BlockSpec((d, LANES), const),
            pl.BlockSpec((1, LANES), const),
        ],
        out_specs=[
            pl.BlockSpec((tm, d), row),
            pl.BlockSpec((tm * SUBLANES, LANES), row),
            pl.BlockSpec((tm, LANES), row),
            pl.BlockSpec((tm, LANES), row),
            pl.BlockSpec((8, LANES), const),
        ],
        out_shape=[
            jax.ShapeDtypeStruct((n, d), F32),
            jax.ShapeDtypeStruct((n * SUBLANES, LANES), F32),
            jax.ShapeDtypeStruct((n, LANES), jnp.int32),
            jax.ShapeDtypeStruct((n, LANES), F32),
            jax.ShapeDtypeStruct((8, LANES), F32),
        ],
        compiler_params=pltpu.CompilerParams(dimension_semantics=("arbitrary",), vmem_limit_bytes=VMEM_LIMIT),
        name="outproj_router",
    )(o_sb, o_sw, o_mb, x, mods, g_group.reshape(1, d), w_out_b, g_ffn.reshape(1, d), w_router_b, b_router_p)


def _moe_kernel(be_ref, nu_ref, xs_ref, wgu_ref, bgu_ref, wd_ref, bd_ref, o_ref, wgu_b, wd_b):
    i = pl.program_id(0)
    d_ff = wd_b.shape[0]

    @pl.when(i < nu_ref[0])
    def _():
        changed = jnp.logical_or(i == 0, be_ref[i] != be_ref[jnp.maximum(i - 1, 0)])

        @pl.when(changed)
        def _():
            wgu_b[...] = wgu_ref[0, 0].astype(BF16)
            wd_b[...] = wd_ref[0, 0].astype(BF16)

        x = _from_row_tiles(xs_ref, MOE_ROWS).astype(BF16)
        gu = _dot(x, wgu_b[...]) + bgu_ref[0, 0]
        gate = jnp.minimum(gu[:, :d_ff], SWIGLU_LIMIT)
        up = jnp.clip(gu[:, d_ff:], -SWIGLU_LIMIT, SWIGLU_LIMIT)
        act = (up + 1.0) * gate * (1.0 / (1.0 + jnp.exp(-SWIGLU_ALPHA * gate)))
        _to_row_tiles(o_ref, _dot(act.astype(BF16), wd_b[...]) + bd_ref[0, 0])

    @pl.when(i >= nu_ref[0])
    def _():
        o_ref[...] = jnp.zeros_like(o_ref)


def _moe_experts(xs, block_e, n_used, layer, w_gate_up, b_gate_up, w_down, b_down):
    depth, n_exp, d, two_f = w_gate_up.shape
    n_rows = xs.shape[0] // SUBLANES
    d_ff = two_f // 2
    n_blocks = n_rows // MOE_ROWS
    blk = lambda i, be, nu: (jnp.minimum(i, nu[0] - 1), 0)
    grid_spec = pltpu.PrefetchScalarGridSpec(
        num_scalar_prefetch=2,
        grid=(n_blocks,),
        in_specs=[
            pl.BlockSpec((MOE_ROWS * SUBLANES, LANES), blk),
            pl.BlockSpec((1, 1, d, two_f), lambda i, be, nu: (layer, be[i], 0, 0)),
            pl.BlockSpec((1, 1, 1, two_f), lambda i, be, nu: (layer, be[i], 0, 0)),
            pl.BlockSpec((1, 1, d_ff, d), lambda i, be, nu: (layer, be[i], 0, 0)),
            pl.BlockSpec((1, 1, 1, d), lambda i, be, nu: (layer, be[i], 0, 0)),
        ],
        out_specs=pl.BlockSpec((MOE_ROWS * SUBLANES, LANES), lambda i, be, nu: (i, 0)),
        scratch_shapes=[
            pltpu.VMEM((d, two_f), BF16),
            pltpu.VMEM((d_ff, d), BF16),
        ],
    )
    return pl.pallas_call(
        _moe_kernel,
        grid_spec=grid_spec,
        out_shape=jax.ShapeDtypeStruct((n_rows * SUBLANES, LANES), F32),
        compiler_params=pltpu.CompilerParams(dimension_semantics=("arbitrary",), vmem_limit_bytes=VMEM_LIMIT),
        name="moe_experts",
    )(block_e, n_used, xs, w_gate_up, b_gate_up.reshape(depth, n_exp, 1, two_f), w_down,
      b_down.reshape(depth, n_exp, 1, d))


def _combine_kernel(pos_ref, pos_next_ref, ys_hbm, x_ref, tw_ref, m_ref, gfin_ref, o_ref, buf, pos_smem,
                    sem_rows, sem_pos, *, final):
    i = pl.program_id(0)
    n_steps = pl.num_programs(0)
    tm = x_ref.shape[0]
    slot = i % 2

    def row_copy(s, t, k, row):
        src = pl.multiple_of(row * SUBLANES, SUBLANES)
        dst = pl.multiple_of(t * SUBLANES, SUBLANES)
        return pltpu.make_async_copy(ys_hbm.at[pl.ds(src, SUBLANES)], buf.at[s, k, pl.ds(dst, SUBLANES)],
                                     sem_rows.at[s])

    def for_row_groups(s, act):
        def body(g, carry):
            t0 = g * ROW_GROUP
            rows = [[pos_smem[s, t0 + j, k] for k in range(TOP_K)] for j in range(ROW_GROUP)]
            for j in range(ROW_GROUP):
                for k in range(TOP_K):
                    act(row_copy(s, t0 + j, k, rows[j][k]), k % DMA_PRIORITIES)
            return carry

        lax.fori_loop(0, tm // ROW_GROUP, body, 0)

    def start_gather(src_ref, s):
        to_smem = pltpu.make_async_copy(src_ref, pos_smem.at[s], sem_pos)
        to_smem.start()
        to_smem.wait()
        for_row_groups(s, lambda cp, prio: cp.start(priority=prio))

    @pl.when(i == 0)
    def _():
        start_gather(pos_ref, 0)

    @pl.when(i + 1 < n_steps)
    def _():
        start_gather(pos_next_ref, 1 - slot)

    for_row_groups(slot, lambda cp, prio: cp.wait())

    tw = tw_ref[...]
    cols = []
    for s in range(SUBLANES):
        acc = tw[:, 0:1] * buf.at[slot, 0][pl.ds(s, tm, stride=SUBLANES), :]
        for k in range(1, TOP_K):
            acc = acc + tw[:, k:k + 1] * buf.at[slot, k][pl.ds(s, tm, stride=SUBLANES), :]
        cols.append(acc)
    x = x_ref[...] + m_ref[0][5:6] * jnp.concatenate(cols, axis=1)
    if final:
        x = _rms(x, gfin_ref[...])
    o_ref[...] = x


def _moe_combine(ys, pos, x, tw, mods, g_final, seq, final):
    n, d = x.shape
    tm = min(COMBINE_TILE, seq)
    per_batch = seq // tm
    n_tiles = n // tm
    return pl.pallas_call(
        functools.partial(_combine_kernel, final=final),
        grid=(n_tiles,),
        in_specs=[
            pl.BlockSpec((tm, LANES), lambda i: (i, 0)),
            pl.BlockSpec((tm, LANES), lambda i: (jnp.minimum(i + 1, n_tiles - 1), 0)),
            pl.BlockSpec(memory_space=pl.ANY),
            pl.BlockSpec((tm, d), lambda i: (i, 0)),
            pl.BlockSpec((tm, LANES), lambda i: (i, 0)),
            pl.BlockSpec((1, 6, d), lambda i: (i // per_batch, 0, 0)),
            pl.BlockSpec((1, d), lambda i: (0, 0)),
        ],
        out_specs=pl.BlockSpec((tm, d), lambda i: (i, 0)),
        out_shape=jax.ShapeDtypeStruct((n, d), F32),
        scratch_shapes=[pltpu.VMEM((2, TOP_K, tm * SUBLANES, LANES), F32), pltpu.SMEM((2, tm, LANES), jnp.int32),
                        pltpu.SemaphoreType.DMA((2,)), pltpu.SemaphoreType.DMA],
        compiler_params=pltpu.CompilerParams(dimension_semantics=("arbitrary",), vmem_limit_bytes=VMEM_LIMIT),
        name="moe_combine",
    )(pos, pos, ys, x, tw, mods, g_final.reshape(1, d))


def _row_plan(counts, n_tok):
    cnt = counts[0, :N_EXPERTS].astype(jnp.int32)
    padded = (cnt + MOE_ROWS - 1) // MOE_ROWS * MOE_ROWS
    pad_end = jnp.cumsum(padded)
    pad_start = pad_end - padded
    n_blocks = -(-n_tok * TOP_K // MOE_ROWS) + N_EXPERTS
    n_used = (pad_end[-1] // MOE_ROWS).astype(jnp.int32)
    blk_first = jnp.minimum(jnp.arange(n_blocks, dtype=jnp.int32), n_used - 1) * MOE_ROWS
    block_e = jnp.sum(blk_first[:, None] >= pad_end[None, :], axis=1).astype(jnp.int32)
    pad_end0 = jnp.concatenate([jnp.zeros((1,), jnp.int32), pad_end.astype(jnp.int32)])
    pad_start_row = jnp.zeros((1, LANES), F32).at[0, :N_EXPERTS].set(pad_start.astype(F32))
    return pad_end0, pad_start_row, block_e, n_used.reshape(1), n_blocks * MOE_ROWS


def _dispatch_kernel(pend_ref, te_ref, h_hbm, pstart_ref, pos_ref, xs_hbm, carry_ref, zero_ref, hbuf, pos_smem,
                     sem_rows, sem_load, sem_misc):
    i = pl.program_id(0)
    n_steps = pl.num_programs(0)
    tm = te_ref.shape[0]
    blk = MOE_ROWS * SUBLANES
    tile = tm * SUBLANES
    lane = lax.broadcasted_iota(jnp.int32, (tm, LANES), 1)

    def tile_load(step):
        src = pl.multiple_of(step * tile, tile)
        return pltpu.make_async_copy(h_hbm.at[pl.ds(src, tile)], hbuf.at[step % 3], sem_load.at[step % 3])

    @pl.when(i == 0)
    def _():
        tile_load(0).start()

    @pl.when(i + 1 < n_steps)
    def _():
        tile_load(i + 1).start()

    @pl.when(i == 0)
    def _():
        carry_ref[...] = jnp.zeros_like(carry_ref)
        zero_ref[...] = jnp.zeros_like(zero_ref)

        def tail_copy(e):
            end = pl.multiple_of(pend_ref[e + 1] * SUBLANES, blk)
            return pltpu.make_async_copy(zero_ref, xs_hbm.at[pl.ds(end - blk, blk)], sem_misc)

        for e in range(N_EXPERTS):
            @pl.when(pend_ref[e + 1] > pend_ref[e])
            def _():
                tail_copy(e).start()
        for e in range(N_EXPERTS):
            @pl.when(pend_ref[e + 1] > pend_ref[e])
            def _():
                tail_copy(e).wait()

        def spare_copy(b):
            return pltpu.make_async_copy(zero_ref, xs_hbm.at[pl.ds(pl.multiple_of(b * blk, blk), blk)], sem_misc)

        first_spare = pend_ref[N_EXPERTS] // MOE_ROWS
        n_blocks = xs_hbm.shape[0] // blk

        def start_spare(b, carry):
            spare_copy(b).start()
            return carry

        def wait_spare(b, carry):
            spare_copy(b).wait()
            return carry

        lax.fori_loop(first_spare, n_blocks, start_spare, 0)
        lax.fori_loop(first_spare, n_blocks, wait_spare, 0)

    te = te_ref[...]
    hits = [lane == te[:, k:k + 1] for k in range(TOP_K)]
    cnt = jnp.zeros((tm, LANES), F32)
    for k in range(TOP_K):
        cnt = cnt + jnp.where(hits[k], 1.0, 0.0)
    cnt_b = cnt.astype(BF16)
    r = lax.broadcasted_iota(jnp.int32, (tm, tm), 0)
    c = lax.broadcasted_iota(jnp.int32, (tm, tm), 1)
    before = jnp.where(c < r, 1.0, 0.0).astype(BF16)
    base = _dot(before, cnt_b) + (carry_ref[0:1, :] + pstart_ref[...])
    pos = jnp.zeros((tm, LANES), jnp.int32)
    for k in range(TOP_K):
        pk = jnp.sum(jnp.where(hits[k], base, 0.0), axis=-1, keepdims=True).astype(jnp.int32)
        pos = jnp.where(lane == k, pk, pos)
    pos_ref[...] = pos
    carry_ref[...] = carry_ref[...] + _dot(jnp.ones((carry_ref.shape[0], tm), BF16), cnt_b)

    slot = i % 2
    to_smem = pltpu.make_async_copy(pos_ref, pos_smem.at[slot], sem_misc)
    to_smem.start()
    to_smem.wait()

    def row_copy(step, s, t, row):
        src = pl.multiple_of(t * SUBLANES, SUBLANES)
        dst = pl.multiple_of(row * SUBLANES, SUBLANES)
        return pltpu.make_async_copy(hbuf.at[step % 3, pl.ds(src, SUBLANES)], xs_hbm.at[pl.ds(dst, SUBLANES)],
                                     sem_rows.at[s])

    def for_row_groups(step, s, act):
        def body(g, carry):
            t0 = g * ROW_GROUP
            rows = [[pos_smem[s, t0 + j, k] for k in range(TOP_K)] for j in range(ROW_GROUP)]
            for j in range(ROW_GROUP):
                for k in range(TOP_K):
                    act(row_copy(step, s, t0 + j, rows[j][k]), k % DMA_PRIORITIES)
            return carry

        lax.fori_loop(0, tm // ROW_GROUP, body, 0)

    tile_load(i).wait()
    for_row_groups(i, slot, lambda cp, prio: cp.start(priority=prio))

    def drain(step, s):
        for_row_groups(step, s, lambda cp, prio: cp.wait())

    @pl.when(i >= 1)
    def _():
        drain(i - 1, 1 - slot)

    @pl.when(i == n_steps - 1)
    def _():
        drain(i, slot)


def _moe_dispatch(top_e, h, pad_end0, pad_start_row, n_rows, seq):
    n = top_e.shape[0]
    tm = min(DISPATCH_TILE, seq)
    grid_spec = pltpu.PrefetchScalarGridSpec(
        num_scalar_prefetch=1,
        grid=(n // tm,),
        in_specs=[
            pl.BlockSpec((tm, LANES), lambda i, pe: (i, 0)),
            pl.BlockSpec(memory_space=pl.ANY),
            pl.BlockSpec((1, LANES), lambda i, pe: (0, 0)),
        ],
        out_specs=[
            pl.BlockSpec((tm, LANES), lambda i, pe: (i, 0)),
            pl.BlockSpec(memory_space=pl.ANY),
        ],
        scratch_shapes=[
            pltpu.VMEM((8, LANES), F32),
            pltpu.VMEM((MOE_ROWS * SUBLANES, LANES), F32),
            pltpu.VMEM((3, tm * SUBLANES, LANES), F32),
            pltpu.SMEM((2, tm, LANES), jnp.int32),
            pltpu.SemaphoreType.DMA((2,)),
            pltpu.SemaphoreType.DMA((3,)),
            pltpu.SemaphoreType.DMA,
        ],
    )
    return pl.pallas_call(
        _dispatch_kernel,
        grid_spec=grid_spec,
        out_shape=[jax.ShapeDtypeStruct((n, LANES), jnp.int32),
                   jax.ShapeDtypeStruct((n_rows * SUBLANES, LANES), F32)],
        compiler_params=pltpu.CompilerParams(dimension_semantics=("arbitrary",), vmem_limit_bytes=VMEM_LIMIT),
        name="moe_dispatch",
    )(pad_end0, top_e, h, pad_start_row)


def _t5_bucket(dist):
    n = jnp.maximum(dist, 0)
    max_exact = NUM_BUCKETS // 2
    nf = jnp.maximum(n, 1).astype(F32)
    large = max_exact + (jnp.log(nf / max_exact) / math.log(MAX_DISTANCE / max_exact)
                         * (NUM_BUCKETS - max_exact)).astype(jnp.int32)
    large = jnp.minimum(large, NUM_BUCKETS - 1)
    return jnp.where(n < max_exact, n, large)


def _bias_lookup(table, dist):
    onehot = (_t5_bucket(dist)[..., None] == jnp.arange(NUM_BUCKETS)).astype(F32)
    out = jnp.einsum("...k,kh->h...", onehot, table.astype(F32), precision=lax.Precision.HIGHEST)
    return out.astype(F32)


def _bias_tables(rel_bias):
    r = jnp.arange(Q_BLOCK)[:, None]
    j2 = jnp.arange(2 * Q_BLOCK)[None, :]
    swa = _bias_lookup(rel_bias[:, :SWA_HEADS], Q_BLOCK + r - j2)
    tab = rel_bias[:, SWA_HEADS:]
    rb = jnp.arange(MOBA_BLOCK)[:, None]
    jb = jnp.arange(2 * MOBA_BLOCK)[None, :]
    top = _bias_lookup(tab, rb + MOBA_BLOCK - jb)
    far = tab[NUM_BUCKETS - 1]
    return swa, top, far.astype(F32)


def kernel(x, c, w_in, w_out, g_norm_mix, g_norm_ffn, g_group, w_mod, b_mod, swa_sinks, rel_bias,
           w_router, b_router, w_gate_up, b_gate_up, w_down, b_down, g_final):
    batch, seq, d = x.shape
    depth = w_in.shape[0]
    n = batch * seq
    assert seq % MOBA_BLOCK == 0 and seq // MOBA_BLOCK <= LANES and d == SUBLANES * LANES

    mods = _modulation(c, w_mod, b_mod)
    bias_swa, bias_top, bias_far = _bias_tables(rel_bias)
    w_in_b = w_in.astype(BF16)
    w_out_b = w_out.astype(BF16)
    w_router_b = jnp.zeros((depth, d, LANES), BF16).at[:, :, :N_EXPERTS].set(w_router.astype(BF16))
    b_router_p = jnp.full((depth, 1, LANES), NEG, F32).at[:, 0, :N_EXPERTS].set(b_router)

    xf = x.reshape(n, d)
    for l in range(depth):
        proj = _inproj(xf, g_norm_mix[l], mods[l], w_in_b[l], seq)
        o_sb = _sb_attention(proj, batch, seq)
        o_sw = _swa_attention(proj, swa_sinks[l], bias_swa, batch, seq)
        o_mb = _moba_attention(proj, bias_far, bias_top, batch, seq)
        xf, h, top_e, top_w, counts = _outproj_router(o_sb, o_sw, o_mb, xf, mods[l], g_group[l], w_out_b[l],
                                                      g_norm_ffn[l], w_router_b[l], b_router_p[l], seq)
        pad_end0, pad_start_row, block_e, n_used, n_rows = _row_plan(counts, n)
        pos, xs = _moe_dispatch(top_e, h, pad_end0, pad_start_row, n_rows, seq)
        ys = _moe_experts(xs, block_e, n_used, l, w_gate_up, b_gate_up, w_down, b_down)
        xf = _moe_combine(ys, pos, xf, top_w, mods[l], g_final, seq, final=(l == depth - 1))
    return xf.reshape(batch, seq, d)
```

```python
import functools
import math

import jax
import jax.numpy as jnp
from jax import lax
from jax.experimental import pallas as pl
from jax.experimental.pallas import tpu as pltpu

F32 = jnp.float32
BF16 = jnp.bfloat16

HEAD_DIM = 64
SB_HEADS = 4
SWA_HEADS = 8
SWA_KV_HEADS = 2
MOBA_HEADS = 4
SB_W = SB_HEADS * HEAD_DIM
SWA_W = SWA_HEADS * HEAD_DIM
SWA_KV_W = SWA_KV_HEADS * HEAD_DIM
MOBA_W = MOBA_HEADS * HEAD_DIM
Q_BLOCK = 128
WINDOW = 128
MOBA_BLOCK = 256
MOBA_TOPK = 3
NUM_BUCKETS = 32
MAX_DISTANCE = 128
N_EXPERTS = 32
TOP_K = 4
SWIGLU_LIMIT = 7.0
SWIGLU_ALPHA = 1.702
MOE_BLOCK = 256
EPS = 1e-6
NEG = -1e30
ATTN_SCALE = HEAD_DIM ** -0.5

LANES = 128
SUBLANES = 8
COL_SB_Q, COL_SB_K, COL_SB_V = 0, 2, 4
COL_SW_Q, COL_SW_K, COL_SW_V = 6, 10, 11
COL_MB_Q, COL_MB_K, COL_MB_V = 12, 14, 16
IN_W = 18 * LANES

SB_SPAN = 512
SB_QUERIES = 512
MOBA_GROUP = 4
TOKEN_TILE = 512
MOE_ROWS = 512
DISPATCH_TILE = 256
COMBINE_TILE = 256
ROW_GROUP = 8
DMA_PRIORITIES = 2
VMEM_LIMIT = 56 * 1024 * 1024


def _rms(x, g):
    return x * lax.rsqrt(jnp.mean(x * x, axis=-1, keepdims=True) + EPS) * g


def _dot_t(a, b):
    return lax.dot_general(a, b, (((1,), (1,)), ((), ())), preferred_element_type=F32)


def _dot(a, b):
    return jnp.dot(a, b, preferred_element_type=F32)


def _to_row_tiles(ref, x):
    rows = x.shape[0]
    for s in range(SUBLANES):
        ref[pl.ds(s, rows, stride=SUBLANES), :] = x[:, s * LANES:(s + 1) * LANES]


def _from_row_tiles(ref, rows):
    return jnp.concatenate([ref[pl.ds(s, rows, stride=SUBLANES), :] for s in range(SUBLANES)], axis=1)


def _mod_kernel(c_ref, w_ref, b_ref, o_ref):
    c = c_ref[...]
    ca = c * (1.0 / (1.0 + jnp.exp(-c)))
    o_ref[0] = _dot(ca, w_ref[0]) + b_ref[0]


def _modulation(c, w_mod, b_mod):
    depth, d, six_d = w_mod.shape
    b = c.shape[0]
    rows = 8
    c8 = jnp.zeros((rows, d), F32).at[:b].set(c)
    tn = six_d // 6
    out = pl.pallas_call(
        _mod_kernel,
        grid=(depth, six_d // tn),
        in_specs=[
            pl.BlockSpec((rows, d), lambda l, j: (0, 0)),
            pl.BlockSpec((1, d, tn), lambda l, j: (l, 0, j)),
            pl.BlockSpec((1, 1, tn), lambda l, j: (l, 0, j)),
        ],
        out_specs=pl.BlockSpec((1, rows, tn), lambda l, j: (l, 0, j)),
        out_shape=jax.ShapeDtypeStruct((depth, rows, six_d), F32),
        name="adaln_mod",
    )(c8, w_mod, b_mod.reshape(depth, 1, six_d))
    return out[:, :b].reshape(depth, b, 6, d)


def _inproj_kernel(x_ref, g_ref, m_ref, w_ref, o_ref):
    m = m_ref[0]
    h = _rms(x_ref[...], g_ref[...]) * (1.0 + m[1:2]) + m[0:1]
    hb = h.astype(BF16)
    step = 2 * LANES
    for j in range(IN_W // step):
        o_ref[:, j * step:(j + 1) * step] = _dot(hb, w_ref[:, j * step:(j + 1) * step]).astype(BF16)


def _inproj(x, g, mods, w_in_b, seq):
    n, d = x.shape
    tm = min(TOKEN_TILE, seq)
    per_batch = seq // tm
    return pl.pallas_call(
        _inproj_kernel,
        grid=(n // tm,),
        in_specs=[
            pl.BlockSpec((tm, d), lambda i: (i, 0)),
            pl.BlockSpec((1, d), lambda i: (0, 0)),
            pl.BlockSpec((1, 6, d), lambda i: (i // per_batch, 0, 0)),
            pl.BlockSpec((d, IN_W), lambda i: (0, 0)),
        ],
        out_specs=pl.BlockSpec((tm, IN_W), lambda i: (i, 0)),
        out_shape=jax.ShapeDtypeStruct((n, IN_W), BF16),
        compiler_params=pltpu.CompilerParams(dimension_semantics=("arbitrary",), vmem_limit_bytes=VMEM_LIMIT),
        name="norm_inproj",
    )(x, g.reshape(1, d), mods, w_in_b)


def _sb_kernel(q_ref, k_ref, v_ref, o_ref, *, span):
    i = pl.program_id(2)
    tq = q_ref.shape[0]
    n_chunks = span // LANES
    lane = lax.broadcasted_iota(jnp.int32, (tq, LANES), 1)
    tri_r = lax.broadcasted_iota(jnp.int32, (LANES, LANES), 0)
    tri_c = lax.broadcasted_iota(jnp.int32, (LANES, LANES), 1)
    tri = jnp.where(tri_r > tri_c, 1.0, 0.0).astype(BF16)
    q = q_ref[...] * ATTN_SCALE
    qms = [jnp.where((lane >= HEAD_DIM * hh) & (lane < HEAD_DIM * (hh + 1)), q, jnp.zeros_like(q))
           for hh in range(2)]
    q_start = i * tq
    key_off = lax.broadcasted_iota(jnp.int32, (tq, span), 1) - lax.broadcasted_iota(jnp.int32, (tq, span), 0)

    def do_span(sidx, state, masked):
        start = pl.multiple_of(sidx * span, span)
        ks = k_ref[pl.ds(start, span), :]
        vs = v_ref[pl.ds(start, span), :]
        new_state = []
        for hh in range(2):
            run, acc = state[hh]
            z = _dot_t(qms[hh], ks).astype(BF16)
            l1p = jnp.log(1.0 + jnp.exp(-jnp.abs(z)))
            mn = jnp.minimum(z, 0.0)
            log_beta = mn - l1p
            log_1m = (mn - z) - l1p
            parts = [None] * n_chunks
            for c in reversed(range(n_chunks)):
                cols = slice(c * LANES, (c + 1) * LANES)
                lc = log_1m[:, cols]
                if masked:
                    strict = (key_off[:, cols] + (start - q_start)) < 0
                    lc = jnp.where(strict, lc, jnp.zeros_like(lc))
                later = _dot(lc, tri) + run
                a = jnp.exp(log_beta[:, cols].astype(F32) + later)
                if masked:
                    a = jnp.where(strict, a, 0.0)
                parts[c] = a.astype(BF16)
                run = run + jnp.sum(lc.astype(F32), axis=-1, keepdims=True)
            acc = acc + _dot(jnp.concatenate(parts, axis=1), vs)
            new_state.append((run, acc))
        return tuple(new_state)

    init = ((jnp.zeros((tq, 1), F32), jnp.zeros((tq, LANES), F32)),) * 2
    top = q_start // span
    state = do_span(top, init, True)
    state = lax.fori_loop(0, top, lambda jj, st: do_span(top - 1 - jj, st, False), state)
    o_ref[...] = jnp.where(lane < HEAD_DIM, state[0][1], state[1][1])


def _sb_attention(proj, batch, seq):
    n = proj.shape[0]
    span = min(SB_SPAN, seq)
    tq = min(SB_QUERIES, span)
    assert span % tq == 0 and seq % span == 0
    nq = seq // tq
    pairs = SB_HEADS // 2
    return pl.pallas_call(
        functools.partial(_sb_kernel, span=span),
        grid=(batch, pairs, nq),
        in_specs=[
            pl.BlockSpec((tq, LANES), lambda b, p, i: (b * nq + i, COL_SB_Q + p)),
            pl.BlockSpec((seq, LANES), lambda b, p, i: (b, COL_SB_K + p)),
            pl.BlockSpec((seq, LANES), lambda b, p, i: (b, COL_SB_V + p)),
        ],
        out_specs=pl.BlockSpec((tq, LANES), lambda b, p, i: (b * nq + i, p)),
        out_shape=jax.ShapeDtypeStruct((n, SB_W), F32),
        compiler_params=pltpu.CompilerParams(
            dimension_semantics=("arbitrary", "arbitrary", "arbitrary"), vmem_limit_bytes=VMEM_LIMIT),
        name="sb_attention",
    )(proj, proj, proj)


def _swa_kernel(sink_ref, qa_ref, qb_ref, kp_ref, kc_ref, vp_ref, vc_ref, bias_ref, o_ref):
    i = pl.program_id(1)
    kk = jnp.concatenate([kp_ref[...], kc_ref[...]], axis=0)
    vv = jnp.concatenate([vp_ref[...], vc_ref[...]], axis=0)
    group = SWA_HEADS // SWA_KV_HEADS
    rows = group * Q_BLOCK
    r = lax.broadcasted_iota(jnp.int32, (rows, 2 * Q_BLOCK), 0)
    j = lax.broadcasted_iota(jnp.int32, (rows, 2 * Q_BLOCK), 1)
    dist = Q_BLOCK + r % Q_BLOCK - j
    valid = (dist >= 0) & (dist < WINDOW) & ((j >= Q_BLOCK) | (i > 0))
    head_of_row = lax.broadcasted_iota(jnp.int32, (rows, 1), 0) // Q_BLOCK
    for g, q_ref in enumerate((qa_ref, qb_ref)):
        q = q_ref[...] * ATTN_SCALE
        qs = jnp.concatenate([q[:, HEAD_DIM * c:HEAD_DIM * (c + 1)] for c in range(group)], axis=0)
        bias = bias_ref[group * g:group * (g + 1)].reshape(rows, 2 * Q_BLOCK)
        s = _dot_t(qs, kk[:, HEAD_DIM * g:HEAD_DIM * (g + 1)]) + bias
        s = jnp.where(valid, s, NEG)
        sink = jnp.zeros((rows, 1), F32)
        for c in range(group):
            sink = jnp.where(head_of_row == c, sink_ref[group * g + c], sink)
        m = jnp.maximum(jnp.max(s, axis=-1, keepdims=True), sink)
        p = jnp.exp(s - m)
        l = jnp.sum(p, axis=-1, keepdims=True) + jnp.exp(sink - m)
        o = _dot(p.astype(BF16), vv[:, HEAD_DIM * g:HEAD_DIM * (g + 1)]) / l
        for c in range(group):
            h = group * g + c
            o_ref[:, HEAD_DIM * h:HEAD_DIM * (h + 1)] = o[Q_BLOCK * c:Q_BLOCK * (c + 1)]


def _swa_attention(proj, sinks, bias, batch, seq):
    n = proj.shape[0]
    nq = seq // Q_BLOCK
    wide = 2 * LANES
    grid_spec = pltpu.PrefetchScalarGridSpec(
        num_scalar_prefetch=1,
        grid=(batch, nq),
        in_specs=[
            pl.BlockSpec((Q_BLOCK, wide), lambda b, i, s: (b * nq + i, COL_SW_Q // 2)),
            pl.BlockSpec((Q_BLOCK, wide), lambda b, i, s: (b * nq + i, COL_SW_Q // 2 + 1)),
            pl.BlockSpec((Q_BLOCK, LANES), lambda b, i, s: (b * nq + jnp.maximum(i - 1, 0), COL_SW_K)),
            pl.BlockSpec((Q_BLOCK, LANES), lambda b, i, s: (b * nq + i, COL_SW_K)),
            pl.BlockSpec((Q_BLOCK, LANES), lambda b, i, s: (b * nq + jnp.maximum(i - 1, 0), COL_SW_V)),
            pl.BlockSpec((Q_BLOCK, LANES), lambda b, i, s: (b * nq + i, COL_SW_V)),
            pl.BlockSpec((SWA_HEADS, Q_BLOCK, 2 * Q_BLOCK), lambda b, i, s: (0, 0, 0)),
        ],
        out_specs=pl.BlockSpec((Q_BLOCK, SWA_W), lambda b, i, s: (b * nq + i, 0)),
    )
    return pl.pallas_call(
        _swa_kernel,
        grid_spec=grid_spec,
        out_shape=jax.ShapeDtypeStruct((n, SWA_W), F32),
        compiler_params=pltpu.CompilerParams(dimension_semantics=("arbitrary", "arbitrary")),
        name="swa_attention",
    )(sinks, proj, proj, proj, proj, proj, proj, bias)


def _moba_kernel(far_ref, q_ref, k_ref, v_ref, btop_ref, o_ref, kmean_ref, rhs_ref, *, group):
    p = pl.program_id(1)
    cur = pl.program_id(2)
    seq = k_ref.shape[0]
    tq = q_ref.shape[0]
    nb = seq // MOBA_BLOCK
    top_w = 2 * MOBA_BLOCK
    far_w = group * MOBA_BLOCK
    lane = lax.broadcasted_iota(jnp.int32, (tq, LANES), 1)

    @pl.when(cur == 0)
    def _():
        blk = lax.broadcasted_iota(jnp.int32, (LANES, seq), 0)
        pos = lax.broadcasted_iota(jnp.int32, (LANES, seq), 1)
        lo = blk * MOBA_BLOCK
        avg = jnp.where((pos >= lo) & (pos < lo + MOBA_BLOCK), 1.0 / MOBA_BLOCK, 0.0).astype(BF16)
        kmean_ref[...] = _dot(avg, k_ref[...]).astype(BF16)
        key_blk = lax.broadcasted_iota(jnp.int32, (seq, LANES), 0) // MOBA_BLOCK
        blk_lane = lax.broadcasted_iota(jnp.int32, (seq, LANES), 1)
        rhs_ref[:, :LANES] = k_ref[...]
        rhs_ref[:, LANES:] = jnp.where(key_blk == blk_lane, 1.0, 0.0).astype(BF16)

    q = q_ref[...]
    kmean = kmean_ref[...]
    lane_f = lane.astype(F32)

    prev_start = pl.multiple_of(jnp.maximum(cur - 1, 0) * MOBA_BLOCK, MOBA_BLOCK)
    own_start = pl.multiple_of(cur * MOBA_BLOCK, MOBA_BLOCK)
    rhs_top = jnp.concatenate([rhs_ref[pl.ds(prev_start, MOBA_BLOCK), :], rhs_ref[pl.ds(own_start, MOBA_BLOCK), :]],
                              axis=0)
    v_top = jnp.concatenate([v_ref[pl.ds(prev_start, MOBA_BLOCK), :], v_ref[pl.ds(own_start, MOBA_BLOCK), :]], axis=0)
    jt = lax.broadcasted_iota(jnp.int32, (tq, top_w), 1)
    rt = lax.broadcasted_iota(jnp.int32, (tq, top_w), 0)
    top_ok = jnp.logical_and(jt < MOBA_BLOCK, cur >= 1) | ((jt >= MOBA_BLOCK) & (rt >= jt - MOBA_BLOCK))

    def update(state, s, vb):
        m, l, acc = state
        m_new = jnp.maximum(m, jnp.max(s, axis=-1, keepdims=True))
        alpha = jnp.exp(m - m_new)
        pexp = jnp.exp(s - m_new)
        l = alpha * l + jnp.sum(pexp, axis=-1, keepdims=True)
        acc = alpha * acc + _dot(pexp.astype(BF16), vb)
        return m_new, l, acc

    lhs_far, states = [], []
    for hh in range(2):
        hmask = (lane >= HEAD_DIM * hh) & (lane < HEAD_DIM * (hh + 1))
        qg = jnp.where(hmask, q, jnp.zeros_like(q))
        qm = qg * ATTN_SCALE
        gate = jnp.where(lane < cur, _dot_t(qg, kmean), NEG)
        sel = jnp.zeros((tq, LANES), jnp.bool_)
        for _ in range(min(MOBA_TOPK, nb)):
            mx = jnp.max(gate, axis=-1, keepdims=True)
            first = jnp.min(jnp.where(gate == mx, lane_f, float(LANES)), axis=-1, keepdims=True)
            hit = lane_f == first
            sel = sel | hit
            gate = jnp.where(hit, -jnp.inf, gate)
        open_top = (sel & (lane == cur - 1)) | (lane == cur)
        open_far = sel & (lane < cur - 1)
        lhs_top = jnp.concatenate([qm, jnp.where(open_top, 0.0, NEG).astype(BF16)], axis=1)
        s = _dot_t(lhs_top, rhs_top) + btop_ref[hh]
        s = jnp.where(top_ok, s, NEG)
        m = jnp.max(s, axis=-1, keepdims=True)
        pexp = jnp.exp(s - m)
        l = jnp.sum(pexp, axis=-1, keepdims=True)
        acc = _dot(pexp.astype(BF16), v_top)
        lhs_far.append(jnp.concatenate([qm, jnp.where(open_far, 0.0, NEG).astype(BF16)], axis=1))
        states.append((m, l, acc))

    def far_group(g, states):
        start = pl.multiple_of(g * far_w, far_w)
        rhs = rhs_ref[pl.ds(start, far_w), :]
        vg = v_ref[pl.ds(start, far_w), :]
        return tuple(update(states[hh], _dot_t(lhs_far[hh], rhs) + far_ref[2 * p + hh], vg) for hh in range(2))

    n_far = jnp.maximum(cur - 1, 0)
    states = lax.fori_loop(0, (n_far + group - 1) // group, far_group, tuple(states))
    outs = [acc / l for (_, l, acc) in states]
    o_ref[...] = jnp.where(lane < HEAD_DIM, outs[0], outs[1])


def _moba_attention(proj, far_bias, bias_top, batch, seq):
    n = proj.shape[0]
    nq = seq // MOBA_BLOCK
    pairs = MOBA_HEADS // 2
    nb = seq // MOBA_BLOCK
    group = min(MOBA_GROUP, nb)
    assert nb % group == 0
    grid_spec = pltpu.PrefetchScalarGridSpec(
        num_scalar_prefetch=1,
        grid=(batch, pairs, nq),
        in_specs=[
            pl.BlockSpec((MOBA_BLOCK, LANES), lambda b, p, i, f: (b * nq + i, COL_MB_Q + p)),
            pl.BlockSpec((seq, LANES), lambda b, p, i, f: (b, COL_MB_K + p)),
            pl.BlockSpec((seq, LANES), lambda b, p, i, f: (b, COL_MB_V + p)),
            pl.BlockSpec((2, MOBA_BLOCK, 2 * MOBA_BLOCK), lambda b, p, i, f: (p, 0, 0)),
        ],
        out_specs=pl.BlockSpec((MOBA_BLOCK, LANES), lambda b, p, i, f: (b * nq + i, p)),
        scratch_shapes=[pltpu.VMEM((LANES, LANES), BF16), pltpu.VMEM((seq, 2 * LANES), BF16)],
    )
    return pl.pallas_call(
        functools.partial(_moba_kernel, group=group),
        grid_spec=grid_spec,
        out_shape=jax.ShapeDtypeStruct((n, MOBA_W), F32),
        compiler_params=pltpu.CompilerParams(
            dimension_semantics=("arbitrary", "arbitrary", "arbitrary"), vmem_limit_bytes=VMEM_LIMIT),
        name="moba_attention",
    )(far_bias, proj, proj, proj, bias_top)


def _outproj_kernel(osb_ref, osw_ref, omb_ref, x_ref, m_ref, gg_ref, wo_ref, gf_ref, wr_ref, br_ref,
                    xo_ref, h_ref, te_ref, tw_ref, cnt_ref):
    m = m_ref[0]

    @pl.when(pl.program_id(0) == 0)
    def _():
        cnt_ref[...] = jnp.zeros_like(cnt_ref)

    c1, c2 = SB_W, SB_W + SWA_W
    y = _dot(_rms(osb_ref[...], gg_ref[:, :c1]).astype(BF16), wo_ref[:c1, :])
    y = y + _dot(_rms(osw_ref[...], gg_ref[:, c1:c2]).astype(BF16), wo_ref[c1:c2, :])
    y = y + _dot(_rms(omb_ref[...], gg_ref[:, c2:]).astype(BF16), wo_ref[c2:, :])
    x = x_ref[...] + m[2:3] * y
    xo_ref[...] = x
    h = _rms(x, gf_ref[...]) * (1.0 + m[4:5]) + m[3:4]
    _to_row_tiles(h_ref, h)
    logits = _dot(h.astype(BF16), wr_ref[...]) + br_ref[...]
    lane = lax.broadcasted_iota(jnp.int32, logits.shape, 1)
    lane_f = lane.astype(F32)
    ids = jnp.zeros(logits.shape, F32)
    wts = jnp.zeros(logits.shape, F32)
    chosen = jnp.zeros(logits.shape, F32)
    top = None
    denom = None
    for r in range(TOP_K):
        mx = jnp.max(logits, axis=-1, keepdims=True)
        first = jnp.min(jnp.where(logits == mx, lane_f, float(LANES)), axis=-1, keepdims=True)
        hit = lane_f == first
        logits = jnp.where(hit, -jnp.inf, logits)
        chosen = jnp.where(hit, 1.0, chosen)
        if r == 0:
            top = mx
        e = jnp.exp(mx - top)
        denom = e if r == 0 else denom + e
        ids = jnp.where(lane == r, first, ids)
        wts = jnp.where(lane == r, e, wts)
    te_ref[...] = ids.astype(jnp.int32)
    tw_ref[...] = wts / denom
    cnt_ref[...] += _dot(jnp.ones((cnt_ref.shape[0], chosen.shape[0]), BF16), chosen.astype(BF16))


def _outproj_router(o_sb, o_sw, o_mb, x, mods, g_group, w_out_b, g_ffn, w_router_b, b_router_p, seq):
    n, d = x.shape
    tm = min(TOKEN_TILE, seq)
    per_batch = seq // tm
    row = lambda i: (i, 0)
    const = lambda i: (0, 0)
    return pl.pallas_call(
        _outproj_kernel,
        grid=(n // tm,),
        in_specs=[
            pl.BlockSpec((tm, SB_W), row),
            pl.BlockSpec((tm, SWA_W), row),
            pl.BlockSpec((tm, MOBA_W), row),
            pl.BlockSpec((tm, d), row),
            pl.BlockSpec((1, 6, d), lambda i: (i // per_batch, 0, 0)),
            pl.BlockSpec((1, d), const),
            pl.BlockSpec((d, d), const),
            pl.BlockSpec((1, d), const),
            pl.BlockSpec((d, LANES), const),
            pl.BlockSpec((1, LANES), const),
        ],
        out_specs=[
            pl.BlockSpec((tm, d), row),
            pl.BlockSpec((tm * SUBLANES, LANES), row),
            pl.BlockSpec((tm, LANES), row),
            pl.BlockSpec((tm, LANES), row),
            pl.BlockSpec((8, LANES), const),
        ],
        out_shape=[
            jax.ShapeDtypeStruct((n, d), F32),
            jax.ShapeDtypeStruct((n * SUBLANES, LANES), F32),
            jax.ShapeDtypeStruct((n, LANES), jnp.int32),
            jax.ShapeDtypeStruct((n, LANES), F32),
            jax.ShapeDtypeStruct((8, LANES), F32),
        ],
        compiler_params=pltpu.CompilerParams(dimension_semantics=("arbitrary",), vmem_limit_bytes=VMEM_LIMIT),
        name="outproj_router",
    )(o_sb, o_sw, o_mb, x, mods, g_group.reshape(1, d), w_out_b, g_ffn.reshape(1, d), w_router_b, b_router_p)


def _moe_kernel(be_ref, nu_ref, xs_ref, wgu_ref, bgu_ref, wd_ref, bd_ref, o_ref, wgu_b, wd_b):
    i = pl.program_id(0)
    d_ff = wd_b.shape[0]

    @pl.when(i < nu_ref[0])
    def _():
        changed = jnp.logical_or(i == 0, be_ref[i] != be_ref[jnp.maximum(i - 1, 0)])

        @pl.when(changed)
        def _():
            wgu_b[...] = wgu_ref[0, 0].astype(BF16)
            wd_b[...] = wd_ref[0, 0].astype(BF16)

        x = _from_row_tiles(xs_ref, MOE_ROWS).astype(BF16)
        gu = _dot(x, wgu_b[...]) + bgu_ref[0, 0]
        gate = jnp.minimum(gu[:, :d_ff], SWIGLU_LIMIT)
        up = jnp.clip(gu[:, d_ff:], -SWIGLU_LIMIT, SWIGLU_LIMIT)
        act = (up + 1.0) * gate * (1.0 / (1.0 + jnp.exp(-SWIGLU_ALPHA * gate)))
        _to_row_tiles(o_ref, _dot(act.astype(BF16), wd_b[...]) + bd_ref[0, 0])

    @pl.when(i >= nu_ref[0])
    def _():
        o_ref[...] = jnp.zeros_like(o_ref)


def _moe_experts(xs, block_e, n_used, layer, w_gate_up, b_gate_up, w_down, b_down):
    depth, n_exp, d, two_f = w_gate_up.shape
    n_rows = xs.shape[0] // SUBLANES
    d_ff = two_f // 2
    n_blocks = n_rows // MOE_ROWS
    blk = lambda i, be, nu: (jnp.minimum(i, nu[0] - 1), 0)
    grid_spec = pltpu.PrefetchScalarGridSpec(
        num_scalar_prefetch=2,
        grid=(n_blocks,),
        in_specs=[
            pl.BlockSpec((MOE_ROWS * SUBLANES, LANES), blk),
            pl.BlockSpec((1, 1, d, two_f), lambda i, be, nu: (layer, be[i], 0, 0)),
            pl.BlockSpec((1, 1, 1, two_f), lambda i, be, nu: (layer, be[i], 0, 0)),
            pl.BlockSpec((1, 1, d_ff, d), lambda i, be, nu: (layer, be[i], 0, 0)),
            pl.BlockSpec((1, 1, 1, d), lambda i, be, nu: (layer, be[i], 0, 0)),
        ],
        out_specs=pl.BlockSpec((MOE_ROWS * SUBLANES, LANES), lambda i, be, nu: (i, 0)),
        scratch_shapes=[
            pltpu.VMEM((d, two_f), BF16),
            pltpu.VMEM((d_ff, d), BF16),
        ],
    )
    return pl.pallas_call(
        _moe_kernel,
        grid_spec=grid_spec,
        out_shape=jax.ShapeDtypeStruct((n_rows * SUBLANES, LANES), F32),
        compiler_params=pltpu.CompilerParams(dimension_semantics=("arbitrary",), vmem_limit_bytes=VMEM_LIMIT),
        name="moe_experts",
    )(block_e, n_used, xs, w_gate_up, b_gate_up.reshape(depth, n_exp, 1, two_f), w_down,
      b_down.reshape(depth, n_exp, 1, d))


def _combine_kernel(pos_ref, pos_next_ref, ys_hbm, x_ref, tw_ref, m_ref, gfin_ref, o_ref, buf, pos_smem,
                    sem_rows, sem_pos, *, final):
    i = pl.program_id(0)
    n_steps = pl.num_programs(0)
    tm = x_ref.shape[0]
    slot = i % 2

    def row_copy(s, t, k, row):
        src = pl.multiple_of(row * SUBLANES, SUBLANES)
        dst = pl.multiple_of(t * SUBLANES, SUBLANES)
        return pltpu.make_async_copy(ys_hbm.at[pl.ds(src, SUBLANES)], buf.at[s, k, pl.ds(dst, SUBLANES)],
                                     sem_rows.at[s])

    def for_row_groups(s, act):
        def body(g, carry):
            t0 = g * ROW_GROUP
            rows = [[pos_smem[s, t0 + j, k] for k in range(TOP_K)] for j in range(ROW_GROUP)]
            for j in range(ROW_GROUP):
                for k in range(TOP_K):
                    act(row_copy(s, t0 + j, k, rows[j][k]), k % DMA_PRIORITIES)
            return carry

        lax.fori_loop(0, tm // ROW_GROUP, body, 0)

    def start_gather(src_ref, s):
        to_smem = pltpu.make_async_copy(src_ref, pos_smem.at[s], sem_pos)
        to_smem.start()
        to_smem.wait()
        for_row_groups(s, lambda cp, prio: cp.start(priority=prio))

    @pl.when(i == 0)
    def _():
        start_gather(pos_ref, 0)

    @pl.when(i + 1 < n_steps)
    def _():
        start_gather(pos_next_ref, 1 - slot)

    for_row_groups(slot, lambda cp, prio: cp.wait())

    tw = tw_ref[...]
    cols = []
    for s in range(SUBLANES):
        acc = tw[:, 0:1] * buf.at[slot, 0][pl.ds(s, tm, stride=SUBLANES), :]
        for k in range(1, TOP_K):
            acc = acc + tw[:, k:k + 1] * buf.at[slot, k][pl.ds(s, tm, stride=SUBLANES), :]
        cols.append(acc)
    x = x_ref[...] + m_ref[0][5:6] * jnp.concatenate(cols, axis=1)
    if final:
        x = _rms(x, gfin_ref[...])
    o_ref[...] = x


def _moe_combine(ys, pos, x, tw, mods, g_final, seq, final):
    n, d = x.shape
    tm = min(COMBINE_TILE, seq)
    per_batch = seq // tm
    n_tiles = n // tm
    return pl.pallas_call(
        functools.partial(_combine_kernel, final=final),
        grid=(n_tiles,),
        in_specs=[
            pl.BlockSpec((tm, LANES), lambda i: (i, 0)),
            pl.BlockSpec((tm, LANES), lambda i: (jnp.minimum(i + 1, n_tiles - 1), 0)),
            pl.BlockSpec(memory_space=pl.ANY),
            pl.BlockSpec((tm, d), lambda i: (i, 0)),
            pl.BlockSpec((tm, LANES), lambda i: (i, 0)),
            pl.BlockSpec((1, 6, d), lambda i: (i // per_batch, 0, 0)),
            pl.BlockSpec((1, d), lambda i: (0, 0)),
        ],
        out_specs=pl.BlockSpec((tm, d), lambda i: (i, 0)),
        out_shape=jax.ShapeDtypeStruct((n, d), F32),
        scratch_shapes=[pltpu.VMEM((2, TOP_K, tm * SUBLANES, LANES), F32), pltpu.SMEM((2, tm, LANES), jnp.int32),
                        pltpu.SemaphoreType.DMA((2,)), pltpu.SemaphoreType.DMA],
        compiler_params=pltpu.CompilerParams(dimension_semantics=("arbitrary",), vmem_limit_bytes=VMEM_LIMIT),
        name="moe_combine",
    )(pos, pos, ys, x, tw, mods, g_final.reshape(1, d))


def _row_plan(counts, n_tok):
    cnt = counts[0, :N_EXPERTS].astype(jnp.int32)
    padded = (cnt + MOE_ROWS - 1) // MOE_ROWS * MOE_ROWS
    pad_end = jnp.cumsum(padded)
    pad_start = pad_end - padded
    n_blocks = -(-n_tok * TOP_K // MOE_ROWS) + N_EXPERTS
    n_used = (pad_end[-1] // MOE_ROWS).astype(jnp.int32)
    blk_first = jnp.minimum(jnp.arange(n_blocks, dtype=jnp.int32), n_used - 1) * MOE_ROWS
    block_e = jnp.sum(blk_first[:, None] >= pad_end[None, :], axis=1).astype(jnp.int32)
    pad_end0 = jnp.concatenate([jnp.zeros((1,), jnp.int32), pad_end.astype(jnp.int32)])
    pad_start_row = jnp.zeros((1, LANES), F32).at[0, :N_EXPERTS].set(pad_start.astype(F32))
    return pad_end0, pad_start_row, block_e, n_used.reshape(1), n_blocks * MOE_ROWS


def _dispatch_kernel(pend_ref, te_ref, h_hbm, pstart_ref, pos_ref, xs_hbm, carry_ref, zero_ref, hbuf, pos_smem,
                     sem_rows, sem_load, sem_misc):
    i = pl.program_id(0)
    n_steps = pl.num_programs(0)
    tm = te_ref.shape[0]
    blk = MOE_ROWS * SUBLANES
    tile = tm * SUBLANES
    lane = lax.broadcasted_iota(jnp.int32, (tm, LANES), 1)

    def tile_load(step):
        src = pl.multiple_of(step * tile, tile)
        return pltpu.make_async_copy(h_hbm.at[pl.ds(src, tile)], hbuf.at[step % 3], sem_load.at[step % 3])

    @pl.when(i == 0)
    def _():
        tile_load(0).start()

    @pl.when(i + 1 < n_steps)
    def _():
        tile_load(i + 1).start()

    @pl.when(i == 0)
    def _():
        carry_ref[...] = jnp.zeros_like(carry_ref)
        zero_ref[...] = jnp.zeros_like(zero_ref)

        def tail_copy(e):
            end = pl.multiple_of(pend_ref[e + 1] * SUBLANES, blk)
            return pltpu.make_async_copy(zero_ref, xs_hbm.at[pl.ds(end - blk, blk)], sem_misc)

        for e in range(N_EXPERTS):
            @pl.when(pend_ref[e + 1] > pend_ref[e])
            def _():
                tail_copy(e).start()
        for e in range(N_EXPERTS):
            @pl.when(pend_ref[e + 1] > pend_ref[e])
            def _():
                tail_copy(e).wait()

        def spare_copy(b):
            return pltpu.make_async_copy(zero_ref, xs_hbm.at[pl.ds(pl.multiple_of(b * blk, blk), blk)], sem_misc)

        first_spare = pend_ref[N_EXPERTS] // MOE_ROWS
        n_blocks = xs_hbm.shape[0] // blk

        def start_spare(b, carry):
            spare_copy(b).start()
            return carry

        def wait_spare(b, carry):
            spare_copy(b).wait()
            return carry

        lax.fori_loop(first_spare, n_blocks, start_spare, 0)
        lax.fori_loop(first_spare, n_blocks, wait_spare, 0)

    te = te_ref[...]
    hits = [lane == te[:, k:k + 1] for k in range(TOP_K)]
    cnt = jnp.zeros((tm, LANES), F32)
    for k in range(TOP_K):
        cnt = cnt + jnp.where(hits[k], 1.0, 0.0)
    cnt_b = cnt.astype(BF16)
    r = lax.broadcasted_iota(jnp.int32, (tm, tm), 0)
    c = lax.broadcasted_iota(jnp.int32, (tm, tm), 1)
    before = jnp.where(c < r, 1.0, 0.0).astype(BF16)
    base = _dot(before, cnt_b) + (carry_ref[0:1, :] + pstart_ref[...])
    pos = jnp.zeros((tm, LANES), jnp.int32)
    for k in range(TOP_K):
        pk = jnp.sum(jnp.where(hits[k], base, 0.0), axis=-1, keepdims=True).astype(jnp.int32)
        pos = jnp.where(lane == k, pk, pos)
    pos_ref[...] = pos
    carry_ref[...] = carry_ref[...] + _dot(jnp.ones((carry_ref.shape[0], tm), BF16), cnt_b)

    slot = i % 2
    to_smem = pltpu.make_async_copy(pos_ref, pos_smem.at[slot], sem_misc)
    to_smem.start()
    to_smem.wait()

    def row_copy(step, s, t, row):
        src = pl.multiple_of(t * SUBLANES, SUBLANES)
        dst = pl.multiple_of(row * SUBLANES, SUBLANES)
        return pltpu.make_async_copy(hbuf.at[step % 3, pl.ds(src, SUBLANES)], xs_hbm.at[pl.ds(dst, SUBLANES)],
                                     sem_rows.at[s])

    def for_row_groups(step, s, act):
        def body(g, carry):
            t0 = g * ROW_GROUP
            rows = [[pos_smem[s, t0 + j, k] for k in range(TOP_K)] for j in range(ROW_GROUP)]
            for j in range(ROW_GROUP):
                for k in range(TOP_K):
                    act(row_copy(step, s, t0 + j, rows[j][k]), k % DMA_PRIORITIES)
            return carry

        lax.fori_loop(0, tm // ROW_GROUP, body, 0)

    tile_load(i).wait()
    for_row_groups(i, slot, lambda cp, prio: cp.start(priority=prio))

    def drain(step, s):
        for_row_groups(step, s, lambda cp, prio: cp.wait())

    @pl.when(i >= 1)
    def _():
        drain(i - 1, 1 - slot)

    @pl.when(i == n_steps - 1)
    def _():
        drain(i, slot)


def _moe_dispatch(top_e, h, pad_end0, pad_start_row, n_rows, seq):
    n = top_e.shape[0]
    tm = min(DISPATCH_TILE, seq)
    grid_spec = pltpu.PrefetchScalarGridSpec(
        num_scalar_prefetch=1,
        grid=(n // tm,),
        in_specs=[
            pl.BlockSpec((tm, LANES), lambda i, pe: (i, 0)),
            pl.BlockSpec(memory_space=pl.ANY),
            pl.BlockSpec((1, LANES), lambda i, pe: (0, 0)),
        ],
        out_specs=[
            pl.BlockSpec((tm, LANES), lambda i, pe: (i, 0)),
            pl.BlockSpec(memory_space=pl.ANY),
        ],
        scratch_shapes=[
            pltpu.VMEM((8, LANES), F32),
            pltpu.VMEM((MOE_ROWS * SUBLANES, LANES), F32),
            pltpu.VMEM((3, tm * SUBLANES, LANES), F32),
            pltpu.SMEM((2, tm, LANES), jnp.int32),
            pltpu.SemaphoreType.DMA((2,)),
            pltpu.SemaphoreType.DMA((3,)),
            pltpu.SemaphoreType.DMA,
        ],
    )
    return pl.pallas_call(
        _dispatch_kernel,
        grid_spec=grid_spec,
        out_shape=[jax.ShapeDtypeStruct((n, LANES), jnp.int32),
                   jax.ShapeDtypeStruct((n_rows * SUBLANES, LANES), F32)],
        compiler_params=pltpu.CompilerParams(dimension_semantics=("arbitrary",), vmem_limit_bytes=VMEM_LIMIT),
        name="moe_dispatch",
    )(pad_end0, top_e, h, pad_start_row)


def _t5_bucket(dist):
    n = jnp.maximum(dist, 0)
    max_exact = NUM_BUCKETS // 2
    nf = jnp.maximum(n, 1).astype(F32)
    large = max_exact + (jnp.log(nf / max_exact) / math.log(MAX_DISTANCE / max_exact)
                         * (NUM_BUCKETS - max_exact)).astype(jnp.int32)
    large = jnp.minimum(large, NUM_BUCKETS - 1)
    return jnp.where(n < max_exact, n, large)


def _bias_lookup(table, dist):
    onehot = (_t5_bucket(dist)[..., None] == jnp.arange(NUM_BUCKETS)).astype(F32)
    out = jnp.einsum("...k,kh->h...", onehot, table.astype(F32), precision=lax.Precision.HIGHEST)
    return out.astype(F32)


def _bias_tables(rel_bias):
    r = jnp.arange(Q_BLOCK)[:, None]
    j2 = jnp.arange(2 * Q_BLOCK)[None, :]
    swa = _bias_lookup(rel_bias[:, :SWA_HEADS], Q_BLOCK + r - j2)
    tab = rel_bias[:, SWA_HEADS:]
    rb = jnp.arange(MOBA_BLOCK)[:, None]
    jb = jnp.arange(2 * MOBA_BLOCK)[None, :]
    top = _bias_lookup(tab, rb + MOBA_BLOCK - jb)
    far = tab[NUM_BUCKETS - 1]
    return swa, top, far.astype(F32)


def kernel(x, c, w_in, w_out, g_norm_mix, g_norm_ffn, g_group, w_mod, b_mod, swa_sinks, rel_bias,
           w_router, b_router, w_gate_up, b_gate_up, w_down, b_down, g_final):
    batch, seq, d = x.shape
    depth = w_in.shape[0]
    n = batch * seq
    assert seq % MOBA_BLOCK == 0 and seq // MOBA_BLOCK <= LANES and d == SUBLANES * LANES

    mods = _modulation(c, w_mod, b_mod)
    bias_swa, bias_top, bias_far = _bias_tables(rel_bias)
    w_in_b = w_in.astype(BF16)
    w_out_b = w_out.astype(BF16)
    w_router_b = jnp.zeros((depth, d, LANES), BF16).at[:, :, :N_EXPERTS].set(w_router.astype(BF16))
    b_router_p = jnp.full((depth, 1, LANES), NEG, F32).at[:, 0, :N_EXPERTS].set(b_router)

    xf = x.reshape(n, d)
    for l in range(depth):
        proj = _inproj(xf, g_norm_mix[l], mods[l], w_in_b[l], seq)
        o_sb = _sb_attention(proj, batch, seq)
        o_sw = _swa_attention(proj, swa_sinks[l], bias_swa, batch, seq)
        o_mb = _moba_attention(proj, bias_far, bias_top, batch, seq)
        xf, h, top_e, top_w, counts = _outproj_router(o_sb, o_sw, o_mb, xf, mods[l], g_group[l], w_out_b[l],
                                                      g_norm_ffn[l], w_router_b[l], b_router_p[l], seq)
        pad_end0, pad_start_row, block_e, n_used, n_rows = _row_plan(counts, n)
        pos, xs = _moe_dispatch(top_e, h, pad_end0, pad_start_row, n_rows, seq)
        ys = _moe_experts(xs, block_e, n_used, l, w_gate_up, b_gate_up, w_down, b_down)
        xf = _moe_combine(ys, pos, xf, top_w, mods[l], g_final, seq, final=(l == depth - 1))
    return xf.reshape(batch, seq, d)
```
